```python
import math
import jax, jax.numpy as jnp
from jax import lax
import numpy as np

D_MODEL = 1024
BATCH = 2
SEQ = 16384
DEPTH = 2

RET_HEADS = 4
RET_DK = D_MODEL // 8
RET_DV = D_MODEL // 8
RET_QK_WIDTH = RET_HEADS * RET_DK
RET_WIDTH = RET_HEADS * RET_DV
RET_CHUNK = 128
ROPE_BASE = 10000.0
DIFF_HEADS = 4
DIFF_DQK = D_MODEL // 16
DIFF_DV = 2 * DIFF_DQK
DIFF_QK_WIDTH = DIFF_HEADS * 2 * DIFF_DQK
DIFF_WIDTH = DIFF_HEADS * DIFF_DV
ATTN_BLOCK = 128
MIX_WIDTH = RET_WIDTH + DIFF_WIDTH
PROJ_SIZES = (RET_QK_WIDTH, RET_QK_WIDTH, RET_WIDTH, RET_WIDTH,
              DIFF_QK_WIDTH, DIFF_QK_WIDTH, DIFF_WIDTH)
PROJ_WIDTH = sum(PROJ_SIZES)
D_FF = 2816
CONV_WIDTH = 3
LN_EPS = 1e-5
HEAD_NORM_EPS = 1e-5
DEEPNORM_ALPHA = (2.0 * DEPTH) ** 0.25
DEEPNORM_BETA = (8.0 * DEPTH) ** -0.25

kernel_name = "hybrid_retention_diffattn_convffn"


def layer_norm(x, g, b):
    xf = x.astype(jnp.float32)
    mu = xf.mean(-1, keepdims=True)
    var = jnp.square(xf - mu).mean(-1, keepdims=True)
    y = (xf - mu) * lax.rsqrt(var + LN_EPS) * g.astype(jnp.float32) + b.astype(jnp.float32)
    return y.astype(x.dtype)


def rotary(t, pos):
    half = t.shape[-1] // 2
    inv_freq = ROPE_BASE ** (-jnp.arange(half, dtype=jnp.float32) / half)
    ang = pos.astype(jnp.float32)[:, None] * inv_freq[None, :]
    cos = jnp.cos(ang)[None, :, None, :]
    sin = jnp.sin(ang)[None, :, None, :]
    t1, t2 = t[..., :half], t[..., half:]
    return jnp.concatenate([t1 * cos - t2 * sin, t1 * sin + t2 * cos], axis=-1)


def retention(q, k, v):
    B, S, H, Dk = q.shape
    Dv = v.shape[-1]
    C = RET_CHUNK
    n = S // C
    gamma = 1.0 - 2.0 ** (-5.0 - jnp.arange(H, dtype=jnp.float32))
    log_g = jnp.log(gamma)
    idx = jnp.arange(C, dtype=jnp.float32)
    rel = idx[:, None] - idx[None, :]
    d_intra = jnp.where(rel[None] >= 0, jnp.exp(log_g[:, None, None] * jnp.maximum(rel, 0.0)[None]), 0.0)
    q_dec = jnp.exp(log_g[:, None] * (idx + 1.0)[None])
    k_dec = jnp.exp(log_g[:, None] * (C - 1.0 - idx)[None])
    c_dec = jnp.exp(log_g * C)
    k = k * (Dk ** -0.5)

    def to_chunks(t):
        return t.reshape(B, n, C, H, t.shape[-1]).transpose(1, 0, 3, 2, 4)

    def step(state, inp):
        qc, kc, vc = inp
        s = jnp.einsum('bhid,bhjd->bhij', qc, kc) * d_intra[None]
        inner = jnp.einsum('bhij,bhjv->bhiv', s, vc)
        cross = jnp.einsum('bhid,bhdv->bhiv', qc * q_dec[None, :, :, None], state)
        state = state * c_dec[None, :, None, None] + jnp.einsum(
            'bhjd,bhjv->bhdv', kc * k_dec[None, :, :, None], vc)
        return state, inner + cross

    state0 = jnp.zeros((B, H, Dk, Dv), jnp.float32)
    _, out = lax.scan(step, state0, (to_chunks(q), to_chunks(k), to_chunks(v)))
    return out.transpose(1, 0, 3, 2, 4).reshape(B, S, H, Dv)


def diff_attention(q, k, v, lam):
    B, S, H, _, Dq = q.shape
    Dv = v.shape[-1]
    T = ATTN_BLOCK
    n = S // T
    qb = (q.astype(jnp.float32) * (Dq ** -0.5)).reshape(B, n, T, H, 2, Dq).transpose(1, 0, 3, 4, 2, 5)
    kb = k.astype(jnp.float32).reshape(B, n, T, H, 2, Dq).transpose(1, 0, 3, 4, 2, 5)
    vb = v.astype(jnp.float32).reshape(B, n, T, H, Dv).transpose(1, 0, 3, 2, 4)
    offs = jnp.arange(T)

    def one_block(args):
        qi, qblk = args
        qpos = qi * T + offs

        def body(j, carry):
            m, l, acc = carry
            kj = lax.dynamic_index_in_dim(kb, j, 0, keepdims=False)
            vj = lax.dynamic_index_in_dim(vb, j, 0, keepdims=False)
            s = jnp.einsum('bhmqd,bhmkd->bhmqk', qblk, kj)
            kpos = j * T + offs
            s = jnp.where(kpos[None, :] <= qpos[:, None], s, -jnp.inf)
            m_new = jnp.maximum(m, s.max(-1))
            p = jnp.exp(s - m_new[..., None])
            corr = jnp.exp(m - m_new)
            l = l * corr + p.sum(-1)
            acc = acc * corr[..., None] + jnp.einsum('bhmqk,bhkv->bhmqv', p, vj)
            return m_new, l, acc

        init = (jnp.full((B, H, 2, T), -jnp.inf, jnp.float32),
                jnp.zeros((B, H, 2, T), jnp.float32),
                jnp.zeros((B, H, 2, T, Dv), jnp.float32))
        _, l, acc = lax.fori_loop(0, qi + 1, body, init)
        o = acc / l[..., None]
        return o[:, :, 0] - lam * o[:, :, 1]

    out = lax.map(one_block, (jnp.arange(n), qb))
    return out.transpose(1, 0, 3, 2, 4).reshape(B, S, H, Dv)


def group_norm_heads(y, g):
    mu = y.mean(-1, keepdims=True)
    var = jnp.square(y - mu).mean(-1, keepdims=True)
    yn = (y - mu) * lax.rsqrt(var + HEAD_NORM_EPS)
    return yn.reshape(y.shape[0], y.shape[1], -1) * g.astype(jnp.float32)


def rms_norm_heads(y, g):
    yn = y * lax.rsqrt(jnp.square(y).mean(-1, keepdims=True) + HEAD_NORM_EPS)
    return yn.reshape(y.shape[0], y.shape[1], -1) * g.astype(jnp.float32)


def causal_dwconv(u, w, b):
    C = u.shape[-1]
    y = lax.conv_general_dilated(u, w[:, None, :], window_strides=(1,),
                                 padding=[(CONV_WIDTH - 1, 0)],
                                 dimension_numbers=('NWC', 'WIO', 'NWC'),
                                 feature_group_count=C)
    return y + b


def mixer(h, w_in, ret_gn_g, diff_norm_g, lam_q1, lam_k1, lam_q2, lam_k2, w_out, lam_init):
    B, S, _ = h.shape
    proj = h @ w_in
    cuts = list(np.cumsum(PROJ_SIZES)[:-1])
    rq, rk, rv, rg, dq, dk, dv = jnp.split(proj, cuts, axis=-1)
    pos = jnp.arange(S)
    rq = rotary(rq.astype(jnp.float32).reshape(B, S, RET_HEADS, RET_DK), pos)
    rk = rotary(rk.astype(jnp.float32).reshape(B, S, RET_HEADS, RET_DK), pos)
    rv = rv.astype(jnp.float32).reshape(B, S, RET_HEADS, RET_DV)
    ret = retention(rq, rk, rv)
    ret = group_norm_heads(ret, ret_gn_g) * jax.nn.silu(rg.astype(jnp.float32))
    lam = (jnp.exp(jnp.sum(lam_q1.astype(jnp.float32) * lam_k1.astype(jnp.float32)))
           - jnp.exp(jnp.sum(lam_q2.astype(jnp.float32) * lam_k2.astype(jnp.float32)))
           + lam_init)
    dq = dq.reshape(B, S, DIFF_HEADS, 2, DIFF_DQK)
    dk = dk.reshape(B, S, DIFF_HEADS, 2, DIFF_DQK)
    dv = dv.reshape(B, S, DIFF_HEADS, DIFF_DV)
    dif = diff_attention(dq, dk, dv, lam)
    dif = rms_norm_heads(dif, diff_norm_g) * (1.0 - lam_init)
    cat = jnp.concatenate([ret, dif], axis=-1).astype(h.dtype)
    return cat @ w_out


def conv_ffn(h, w_up, conv_w, conv_b, w_down):
    u = causal_dwconv(h @ w_up, conv_w, conv_b)
    a, b = jnp.split(u, 2, axis=-1)
    return (jax.nn.silu(a) * b) @ w_down


def setup_inputs(seed: int = 0) -> dict:
    key = jax.random.key(seed)
    ks = jax.random.split(key, 20)
    L = DEPTH
    f32 = jnp.float32
    nrm = lambda k, shape: jax.random.normal(k, shape, f32)
    return {
        "x": nrm(ks[0], (BATCH, SEQ, D_MODEL)),
        "w_in": nrm(ks[1], (L, D_MODEL, PROJ_WIDTH)) * D_MODEL ** -0.5,
        "ret_gn_g": 1.0 + 0.02 * nrm(ks[2], (L, RET_WIDTH)),
        "diff_norm_g": 1.0 + 0.02 * nrm(ks[3], (L, DIFF_WIDTH)),
        "lam_q1": 0.1 * nrm(ks[4], (L, DIFF_DQK)),
        "lam_k1": 0.1 * nrm(ks[5], (L, DIFF_DQK)),
        "lam_q2": 0.1 * nrm(ks[6], (L, DIFF_DQK)),
        "lam_k2": 0.1 * nrm(ks[7], (L, DIFF_DQK)),
        "w_out": nrm(ks[8], (L, MIX_WIDTH, D_MODEL)) * MIX_WIDTH ** -0.5 * DEEPNORM_BETA,
        "ln1_g": 1.0 + 0.02 * nrm(ks[9], (L, D_MODEL)),
        "ln1_b": 0.02 * nrm(ks[10], (L, D_MODEL)),
        "w_up": nrm(ks[11], (L, D_MODEL, 2 * D_FF)) * D_MODEL ** -0.5,
        "conv_w": nrm(ks[12], (L, CONV_WIDTH, 2 * D_FF)) * CONV_WIDTH ** -0.5,
        "conv_b": 0.02 * nrm(ks[13], (L, 2 * D_FF)),
        "w_down": nrm(ks[14], (L, D_FF, D_MODEL)) * D_FF ** -0.5 * DEEPNORM_BETA,
        "ln2_g": 1.0 + 0.02 * nrm(ks[15], (L, D_MODEL)),
        "ln2_b": 0.02 * nrm(ks[16], (L, D_MODEL)),
    }


def reference(x, w_in, ret_gn_g, diff_norm_g, lam_q1, lam_k1, lam_q2, lam_k2, w_out,
              ln1_g, ln1_b, w_up, conv_w, conv_b, w_down, ln2_g, ln2_b):
    for l in range(DEPTH):
        lam_init = 0.8 - 0.6 * math.exp(-0.3 * l)
        mix = mixer(x, w_in[l], ret_gn_g[l], diff_norm_g[l], lam_q1[l], lam_k1[l],
                    lam_q2[l], lam_k2[l], w_out[l], lam_init)
        x = layer_norm(DEEPNORM_ALPHA * x + mix, ln1_g[l], ln1_b[l])
        ffn = conv_ffn(x, w_up[l], conv_w[l], conv_b[l], w_down[l])
        x = layer_norm(DEEPNORM_ALPHA * x + ffn, ln2_g[l], ln2_b[l])
    return x
```

```python
import functools
import math

import jax
import jax.numpy as jnp
from jax import lax
from jax.experimental import pallas as pl
from jax.experimental.pallas import tpu as pltpu

F32 = jnp.float32
BF16 = jnp.bfloat16

D_MODEL = 1024
DEPTH = 2
HEADS = 4
HEAD_W = 128
GROUP_W = HEADS * HEAD_W
RET_CHUNK = 128
DIFF_DQK = 64
D_FF = 2816
ROPE_BASE = 10000.0
LN_EPS = 1e-5
HEAD_NORM_EPS = 1e-5
DEEPNORM_ALPHA = (2.0 * DEPTH) ** 0.25

V7X_VMEM_LIMIT = 56 * 1024 * 1024

PROJ_ROWS = 512
RET_ROWS = 512
ATT_TQ = 512
ATT_TK = PROJ_ROWS
OUT_ROWS = 512
FFN_ROWS = 512
FFN_COLS = 1408
HALO = 16

NT_DIMS = (((1,), (1,)), ((), ()))


def _dot(a, b):
    return jnp.dot(a, b, preferred_element_type=F32)


def _dot_nt(a, b):
    return lax.dot_general(a, b, NT_DIMS, preferred_element_type=F32)


def _layer_norm(y, g, b):
    mu = jnp.mean(y, axis=-1, keepdims=True)
    yc = y - mu
    var = jnp.mean(yc * yc, axis=-1, keepdims=True)
    return yc * lax.rsqrt(var + LN_EPS) * g + b


def _proj_kernel(x_ref, cs_ref, sn_ref, cst_ref, snt_ref,
                 wrq_ref, wrkt_ref, wrv_ref, wrg_ref, wdq_ref, wdk_ref, wdvt_ref,
                 rq_ref, rkt_ref, rv_ref, rg_ref, dq_ref, dk_ref, dvt_ref):
    xb = x_ref[0].astype(BF16)
    t = _dot(xb, wrq_ref[...])
    cs = cs_ref[...]
    sn = sn_ref[...]
    for h in range(HEADS):
        th = t[:, h * HEAD_W:(h + 1) * HEAD_W]
        rq_ref[0, :, h * HEAD_W:(h + 1) * HEAD_W] = (
            th * cs + pltpu.roll(th, HEAD_W // 2, 1) * sn).astype(BF16)
    tt = _dot_nt(wrkt_ref[...], xb)
    cst = cst_ref[...]
    snt = snt_ref[...]
    half = HEAD_W // 2
    kscale = HEAD_W ** -0.5
    for h in range(HEADS):
        a = tt[h * HEAD_W:h * HEAD_W + half]
        b = tt[h * HEAD_W + half:(h + 1) * HEAD_W]
        rkt_ref[0, h * HEAD_W:h * HEAD_W + half, :] = ((a * cst - b * snt) * kscale).astype(BF16)
        rkt_ref[0, h * HEAD_W + half:(h + 1) * HEAD_W, :] = ((a * snt + b * cst) * kscale).astype(BF16)
    rv_ref[0] = _dot(xb, wrv_ref[...]).astype(BF16)
    rg_ref[0] = _dot(xb, wrg_ref[...])
    dq_ref[0] = (_dot(xb, wdq_ref[...]) * (DIFF_DQK ** -0.5)).astype(BF16)
    dk_ref[0] = _dot(xb, wdk_ref[...]).astype(BF16)
    dvt = _dot_nt(wdvt_ref[...], xb).astype(BF16)
    dvt_ref[0, :, 0] = dvt.reshape(HEADS, HEAD_W, PROJ_ROWS)


def _proj(x, tabs, w):
    B, S, D = x.shape
    T = PROJ_ROWS
    n = S // T
    cs, sn, cst, snt = tabs
    row = lambda b, i: (b, i, 0)
    const2 = lambda b, i: (0, 0)
    wspec = lambda shape: pl.BlockSpec(shape, const2)
    out_shapes = (
        jax.ShapeDtypeStruct((B, S, GROUP_W), BF16),
        jax.ShapeDtypeStruct((B, GROUP_W, S), BF16),
        jax.ShapeDtypeStruct((B, S, GROUP_W), BF16),
        jax.ShapeDtypeStruct((B, S, GROUP_W), F32),
        jax.ShapeDtypeStruct((B, S, GROUP_W), BF16),
        jax.ShapeDtypeStruct((B, S, GROUP_W), BF16),
        jax.ShapeDtypeStruct((B, HEADS, n, HEAD_W, T), BF16),
    )
    rows = pl.BlockSpec((1, T, GROUP_W), row)
    return pl.pallas_call(
        _proj_kernel,
        grid=(B, n),
        in_specs=[
            pl.BlockSpec((1, T, D), row),
            pl.BlockSpec((T, HEAD_W), lambda b, i: (i, 0)),
            pl.BlockSpec((T, HEAD_W), lambda b, i: (i, 0)),
            pl.BlockSpec((HEAD_W // 2, T), lambda b, i: (0, i)),
            pl.BlockSpec((HEAD_W // 2, T), lambda b, i: (0, i)),
            wspec((D, GROUP_W)), wspec((GROUP_W, D)), wspec((D, GROUP_W)), wspec((D, GROUP_W)),
            wspec((D, GROUP_W)), wspec((D, GROUP_W)), wspec((GROUP_W, D)),
        ],
        out_specs=(
            rows,
            pl.BlockSpec((1, GROUP_W, T), lambda b, i: (b, 0, i)),
            rows, rows, rows, rows,
            pl.BlockSpec((1, HEADS, 1, HEAD_W, T), lambda b, i: (b, 0, i, 0, 0)),
        ),
        out_shape=out_shapes,
        compiler_params=pltpu.CompilerParams(
            dimension_semantics=("arbitrary", "arbitrary"),
            vmem_limit_bytes=V7X_VMEM_LIMIT),
        name="proj",
    )(x, cs, sn, cst, snt, *w)


def _retention_kernel(q_ref, kt_ref, v_ref, g_ref, gn_ref, o_ref, state_ref):
    C = RET_CHUNK

    @pl.when(pl.program_id(1) == 0)
    def _():
        state_ref[...] = jnp.zeros_like(state_ref)

    ri = lax.broadcasted_iota(jnp.int32, (C, C), 0)
    ci = lax.broadcasted_iota(jnp.int32, (C, C), 1)
    rel = (ri - ci).astype(F32)
    gn = gn_ref[...]
    for h in range(HEADS):
        log_g = math.log(1.0 - 2.0 ** (-5.0 - h))
        d_intra = jnp.where(ri >= ci, jnp.exp(log_g * jnp.maximum(rel, 0.0)), 0.0)
        q_dec = jnp.exp(log_g * (ri.astype(F32) + 1.0))
        k_dec = jnp.exp(log_g * (C - 1.0 - ci.astype(F32)))
        c_dec = math.exp(log_g * C)
        cols = slice(h * HEAD_W, (h + 1) * HEAD_W)
        state = state_ref[h]
        for c in range(RET_ROWS // C):
            rws = slice(c * C, (c + 1) * C)
            qc = q_ref[0, rws, cols]
            ktc = kt_ref[0, cols, rws]
            vc = v_ref[0, rws, cols]
            s = _dot(qc, ktc) * d_intra
            inner = _dot(s.astype(BF16), vc)
            cross = _dot((qc.astype(F32) * q_dec).astype(BF16), state.astype(BF16))
            state = state * c_dec + _dot((ktc.astype(F32) * k_dec).astype(BF16), vc)
            y = inner + cross
            mu = jnp.mean(y, axis=-1, keepdims=True)
            yc = y - mu
            var = jnp.mean(yc * yc, axis=-1, keepdims=True)
            yn = yc * lax.rsqrt(var + HEAD_NORM_EPS) * gn[:, cols]
            gate = g_ref[0, rws, cols]
            o_ref[0, rws, cols] = (yn * (gate * jax.nn.sigmoid(gate))).astype(BF16)
        state_ref[h] = state


def _retention(rq, rkt, rv, rg, gn):
    B, S, W = rq.shape
    T = RET_ROWS
    row = lambda b, i: (b, i, 0)
    rows = pl.BlockSpec((1, T, W), row)
    return pl.pallas_call(
        _retention_kernel,
        grid=(B, S // T),
        in_specs=[rows, pl.BlockSpec((1, W, T), lambda b, i: (b, 0, i)), rows, rows,
                  pl.BlockSpec((1, W), lambda b, i: (0, 0))],
        out_specs=rows,
        out_shape=jax.ShapeDtypeStruct((B, S, W), BF16),
        scratch_shapes=[pltpu.VMEM((HEADS, HEAD_W, HEAD_W), F32)],
        compiler_params=pltpu.CompilerParams(
            dimension_semantics=("arbitrary", "arbitrary"),
            vmem_limit_bytes=V7X_VMEM_LIMIT),
        name="retention",
    )(rq, rkt, rv, rg, gn)


def _diffattn_kernel(lq1_ref, lk1_ref, lq2_ref, lk2_ref, q_ref, k_ref, vt_ref, g_ref,
                     o_ref, m_ref, l_ref, acc_ref, *, lam_init):
    TQ, TK = ATT_TQ, ATT_TK
    qi = pl.program_id(2)
    q = q_ref[0]
    lane = lax.broadcasted_iota(jnp.int32, q.shape, 1)
    zero = jnp.zeros_like(q)
    qs = jnp.concatenate([jnp.where(lane < DIFF_DQK, q, zero),
                          jnp.where(lane >= DIFF_DQK, q, zero)], axis=0)

    m_ref[...] = jnp.full_like(m_ref, -1e30)
    l_ref[...] = jnp.zeros_like(l_ref)
    acc_ref[...] = jnp.zeros_like(acc_ref)

    def step(j, masked):
        start = pl.multiple_of(j * TK, TK)
        kj = k_ref[0, pl.ds(start, TK), :]
        st = _dot_nt(kj, qs)
        if masked:
            kpos = j * TK + lax.broadcasted_iota(jnp.int32, st.shape, 0)
            col = lax.broadcasted_iota(jnp.int32, st.shape, 1)
            qpos = qi * TQ + jnp.where(col >= TQ, col - TQ, col)
            st = jnp.where(kpos <= qpos, st, -1e30)
        m_prev = m_ref[...]
        m_new = jnp.maximum(m_prev, jnp.max(st, axis=0, keepdims=True))
        p = jnp.exp(st - m_new)
        corr = jnp.exp(m_prev - m_new)
        l_ref[...] = l_ref[...] * corr + jnp.sum(p, axis=0, keepdims=True)
        acc_ref[...] = acc_ref[...] * corr + _dot(vt_ref[0, 0, j], p.astype(BF16))
        m_ref[...] = m_new

    def body(j, carry):
        step(j, False)
        return carry

    lax.fori_loop(0, qi, body, 0)
    step(qi, True)

    lam = (jnp.exp(jnp.sum(lq1_ref[...] * lk1_ref[...], axis=-1, keepdims=True))
           - jnp.exp(jnp.sum(lq2_ref[...] * lk2_ref[...], axis=-1, keepdims=True))
           + lam_init)
    ot = acc_ref[...] / l_ref[...]
    dt = ot[:, :TQ] - lam * ot[:, TQ:]
    ms = jnp.mean(dt * dt, axis=0, keepdims=True)
    dn = dt * lax.rsqrt(ms + HEAD_NORM_EPS)
    o_ref[0] = (dn.T * g_ref[...] * (1.0 - lam_init)).astype(BF16)


def _diffattn(dq, dk, dvt, lam_vecs, g, lam_init):
    B, S, W = dq.shape
    TQ = ATT_TQ
    nk = dvt.shape[2]
    vec = pl.BlockSpec((1, DIFF_DQK), lambda b, h, i: (0, 0))
    return pl.pallas_call(
        functools.partial(_diffattn_kernel, lam_init=lam_init),
        grid=(B, HEADS, S // TQ),
        in_specs=[vec, vec, vec, vec,
                  pl.BlockSpec((1, TQ, HEAD_W), lambda b, h, i: (b, i, h)),
                  pl.BlockSpec((1, S, HEAD_W), lambda b, h, i: (b, 0, h)),
                  pl.BlockSpec((1, 1, nk, HEAD_W, ATT_TK), lambda b, h, i: (b, h, 0, 0, 0)),
                  pl.BlockSpec((1, HEAD_W), lambda b, h, i: (0, h))],
        out_specs=pl.BlockSpec((1, TQ, HEAD_W), lambda b, h, i: (b, i, h)),
        out_shape=jax.ShapeDtypeStruct((B, S, W), BF16),
        scratch_shapes=[pltpu.VMEM((1, 2 * TQ), F32),
                        pltpu.VMEM((1, 2 * TQ), F32),
                        pltpu.VMEM((HEAD_W, 2 * TQ), F32)],
        compiler_params=pltpu.CompilerParams(
            dimension_semantics=("arbitrary", "arbitrary", "arbitrary"),
            vmem_limit_bytes=V7X_VMEM_LIMIT),
        name="diffattn",
    )(*lam_vecs, dq, dk, dvt, g)


def _outproj_kernel(x_ref, ret_ref, dif_ref, wr_ref, wd_ref, g_ref, b_ref, o_ref):
    mix = _dot(ret_ref[...], wr_ref[...]) + _dot(dif_ref[...], wd_ref[...])
    o_ref[...] = _layer_norm(DEEPNORM_ALPHA * x_ref[...] + mix, g_ref[...], b_ref[...])


def _outproj(x2, ret2, dif2, w_ret, w_dif, g, b):
    R, D = x2.shape
    T = OUT_ROWS
    row = lambda i: (i, 0)
    const = lambda i: (0, 0)
    return pl.pallas_call(
        _outproj_kernel,
        grid=(R // T,),
        in_specs=[pl.BlockSpec((T, D), row),
                  pl.BlockSpec((T, GROUP_W), row), pl.BlockSpec((T, GROUP_W), row),
                  pl.BlockSpec((GROUP_W, D), const), pl.BlockSpec((GROUP_W, D), const),
                  pl.BlockSpec((1, D), const), pl.BlockSpec((1, D), const)],
        out_specs=pl.BlockSpec((T, D), row),
        out_shape=jax.ShapeDtypeStruct((R, D), F32),
        compiler_params=pltpu.CompilerParams(
            dimension_semantics=("arbitrary",), vmem_limit_bytes=V7X_VMEM_LIMIT),
        name="outproj",
    )(x2, ret2, dif2, w_ret, w_dif, g, b)


def _ffn_kernel(x_ref, halo_ref, wa_ref, wb_ref, cwa_ref, cwb_ref, cba_ref, cbb_ref,
                wd_ref, g_ref, b_ref, o_ref, xb_ref, u_ref, acc_ref, *, tiles_per_seq):
    T = FFN_ROWS
    i = pl.program_id(0)
    j = pl.program_id(1)
    first_of_seq = (i % tiles_per_seq) == 0

    @pl.when(j == 0)
    def _():
        xb_ref[0:HALO, :] = halo_ref[...].astype(BF16)
        xb_ref[HALO:, :] = x_ref[...].astype(BF16)

    def conv_half(w_ref, cw_ref, cb_ref):
        u_ref[...] = _dot(xb_ref[...], w_ref[...])

        @pl.when(first_of_seq)
        def _():
            u_ref[0:HALO, :] = jnp.zeros((HALO, FFN_COLS), F32)

        cw = cw_ref[...]
        return (cw[2:3] * u_ref[HALO:HALO + T, :]
                + cw[1:2] * u_ref[HALO - 1:HALO - 1 + T, :]
                + cw[0:1] * u_ref[HALO - 2:HALO - 2 + T, :]
                + cb_ref[...])

    ya = conv_half(wa_ref, cwa_ref, cba_ref)
    yb = conv_half(wb_ref, cwb_ref, cbb_ref)
    gated = (ya * jax.nn.sigmoid(ya) * yb).astype(BF16)
    part = _dot(gated, wd_ref[...])

    @pl.when(j == 0)
    def _():
        acc_ref[...] = part

    @pl.when(j > 0)
    def _():
        acc_ref[...] += part

    @pl.when(j == pl.num_programs(1) - 1)
    def _():
        o_ref[...] = _layer_norm(DEEPNORM_ALPHA * x_ref[...] + acc_ref[...],
                                 g_ref[...], b_ref[...])


def _ffn(x2, w_a, w_b, cw_a, cw_b, cb_a, cb_b, w_down, g, b, seq):
    R, D = x2.shape
    T = FFN_ROWS
    C = FFN_COLS
    tiles_per_seq = seq // T
    const = lambda i, j: (0, 0)
    col = lambda i, j: (0, j)
    return pl.pallas_call(
        functools.partial(_ffn_kernel, tiles_per_seq=tiles_per_seq),
        grid=(R // T, D_FF // C),
        in_specs=[pl.BlockSpec((T, D), lambda i, j: (i, 0)),
                  pl.BlockSpec((HALO, D), lambda i, j: (jnp.maximum(i * (T // HALO) - 1, 0), 0)),
                  pl.BlockSpec((D, C), col), pl.BlockSpec((D, C), col),
                  pl.BlockSpec((3, C), col), pl.BlockSpec((3, C), col),
                  pl.BlockSpec((1, C), col), pl.BlockSpec((1, C), col),
                  pl.BlockSpec((C, D), lambda i, j: (j, 0)),
                  pl.BlockSpec((1, D), const), pl.BlockSpec((1, D), const)],
        out_specs=pl.BlockSpec((T, D), lambda i, j: (i, 0)),
        out_shape=jax.ShapeDtypeStruct((R, D), F32),
        scratch_shapes=[pltpu.VMEM((HALO + T, D), BF16),
                        pltpu.VMEM((HALO + T, C), F32),
                        pltpu.VMEM((T, D), F32)],
        compiler_params=pltpu.CompilerParams(
            dimension_semantics=("arbitrary", "arbitrary"),
            vmem_limit_bytes=V7X_VMEM_LIMIT),
        name="ffn",
    )(x2, x2, w_a, w_b, cw_a, cw_b, cb_a, cb_b, w_down, g, b)


def _rotary_tables(seq):
    half = HEAD_W // 2
    inv_freq = ROPE_BASE ** (-jnp.arange(half, dtype=F32) / half)
    ang = jnp.arange(seq).astype(F32)[:, None] * inv_freq[None, :]
    cos, sin = jnp.cos(ang), jnp.sin(ang)
    return (jnp.concatenate([cos, cos], axis=1), jnp.concatenate([-sin, sin], axis=1),
            cos.T, sin.T)


def kernel(x, w_in, ret_gn_g, diff_norm_g, lam_q1, lam_k1, lam_q2, lam_k2, w_out,
           ln1_g, ln1_b, w_up, conv_w, conv_b, w_down, ln2_g, ln2_b):
    B, S, D = x.shape
    tabs = _rotary_tables(S)
    G = GROUP_W
    for l in range(DEPTH):
        lam_init = 0.8 - 0.6 * math.exp(-0.3 * l)
        wl = w_in[l].astype(BF16)
        w = (wl[:, 0:G], wl[:, G:2 * G].T, wl[:, 2 * G:3 * G], wl[:, 3 * G:4 * G],
             wl[:, 4 * G:5 * G], wl[:, 5 * G:6 * G], wl[:, 6 * G:7 * G].T)
        rq, rkt, rv, rg, dq, dk, dvt = _proj(x, tabs, w)
        ret = _retention(rq, rkt, rv, rg, ret_gn_g[l][None, :])
        lam_vecs = (lam_q1[l][None, :], lam_k1[l][None, :], lam_q2[l][None, :], lam_k2[l][None, :])
        dif = _diffattn(dq, dk, dvt, lam_vecs, diff_norm_g[l][None, :], lam_init)
        wo = w_out[l].astype(BF16)
        x2 = _outproj(x.reshape(B * S, D), ret.reshape(B * S, G), dif.reshape(B * S, G),
                      wo[:G], wo[G:], ln1_g[l][None, :], ln1_b[l][None, :])
        wu = w_up[l].astype(BF16)
        x2 = _ffn(x2, wu[:, :D_FF], wu[:, D_FF:], conv_w[l][:, :D_FF], conv_w[l][:, D_FF:],
                  conv_b[l][None, :D_FF], conv_b[l][None, D_FF:], w_down[l].astype(BF16),
                  ln2_g[l][None, :], ln2_b[l][None, :], S)
        x = x2.reshape(B, S, D)
    return x
```

```python
import functools
import math

import jax
import jax.numpy as jnp
from jax import lax
from jax.experimental import pallas as pl
from jax.experimental.pallas import tpu as pltpu

F32 = jnp.float32
BF16 = jnp.bfloat16

D_MODEL = 1024
DEPTH = 2
HEADS = 4
HEAD_W = 128
GROUP_W = HEADS * HEAD_W
RET_CHUNK = 128
DIFF_DQK = 64
D_FF = 2816
ROPE_BASE = 10000.0
LN_EPS = 1e-5
HEAD_NORM_EPS = 1e-5
DEEPNORM_ALPHA = (2.0 * DEPTH) ** 0.25

V7X_VMEM_LIMIT = 56 * 1024 * 1024

PROJ_ROWS = 512
RET_ROWS = 512
ATT_TQ = 512
ATT_TK = PROJ_ROWS
OUT_ROWS = 512
FFN_ROWS = 512
FFN_COLS = 1408
HALO = 16
ONES_ROWS = 16
DIFF_Q_SCALE = DIFF_DQK ** -0.5 * math.log2(math.e)

NT_DIMS = (((1,), (1,)), ((), ()))


def _dot(a, b):
    return jnp.dot(a, b, preferred_element_type=F32)


def _dot_nt(a, b):
    return lax.dot_general(a, b, NT_DIMS, preferred_element_type=F32)


def _layer_norm(y, g, b):
    mu = jnp.mean(y, axis=-1, keepdims=True)
    yc = y - mu
    var = jnp.mean(yc * yc, axis=-1, keepdims=True)
    return yc * lax.rsqrt(var + LN_EPS) * g + b


def _proj_kernel(x_ref, cs_ref, sn_ref, cst_ref, snt_ref,
                 wrq_ref, wrkt_ref, wrv_ref, wrg_ref, wdq_ref, wdk_ref, wdvt_ref,
                 rq_ref, rkt_ref, rv_ref, rg_ref, dq_ref, dk_ref, dvt_ref):
    xb = x_ref[0].astype(BF16)
    t = _dot(xb, wrq_ref[...])
    cs = cs_ref[...]
    sn = sn_ref[...]
    for h in range(HEADS):
        th = t[:, h * HEAD_W:(h + 1) * HEAD_W]
        rq_ref[0, :, h * HEAD_W:(h + 1) * HEAD_W] = (
            th * cs + pltpu.roll(th, HEAD_W // 2, 1) * sn).astype(BF16)
    tt = _dot_nt(wrkt_ref[...], xb)
    cst = cst_ref[...]
    snt = snt_ref[...]
    half = HEAD_W // 2
    kscale = HEAD_W ** -0.5
    for h in range(HEADS):
        a = tt[h * HEAD_W:h * HEAD_W + half]
        b = tt[h * HEAD_W + half:(h + 1) * HEAD_W]
        rkt_ref[0, h * HEAD_W:h * HEAD_W + half, :] = ((a * cst - b * snt) * kscale).astype(BF16)
        rkt_ref[0, h * HEAD_W + half:(h + 1) * HEAD_W, :] = ((a * snt + b * cst) * kscale).astype(BF16)
    rv_ref[0] = _dot(xb, wrv_ref[...]).astype(BF16)
    rg_ref[0] = _dot(xb, wrg_ref[...])
    dq_ref[0] = (_dot(xb, wdq_ref[...]) * DIFF_Q_SCALE).astype(BF16)
    dk_ref[0] = _dot(xb, wdk_ref[...]).astype(BF16)
    dvt = _dot_nt(wdvt_ref[...], xb).astype(BF16)
    dvt_ref[0, :, 0] = dvt.reshape(HEADS, HEAD_W, PROJ_ROWS)


def _proj(x, tabs, w):
    B, S, D = x.shape
    T = PROJ_ROWS
    n = S // T
    cs, sn, cst, snt = tabs
    row = lambda b, i: (b, i, 0)
    const2 = lambda b, i: (0, 0)
    wspec = lambda shape: pl.BlockSpec(shape, const2)
    out_shapes = (
        jax.ShapeDtypeStruct((B, S, GROUP_W), BF16),
        jax.ShapeDtypeStruct((B, GROUP_W, S), BF16),
        jax.ShapeDtypeStruct((B, S, GROUP_W), BF16),
        jax.ShapeDtypeStruct((B, S, GROUP_W), F32),
        jax.ShapeDtypeStruct((B, S, GROUP_W), BF16),
        jax.ShapeDtypeStruct((B, S, GROUP_W), BF16),
        jax.ShapeDtypeStruct((B, HEADS, n, HEAD_W, T), BF16),
    )
    rows = pl.BlockSpec((1, T, GROUP_W), row)
    return pl.pallas_call(
        _proj_kernel,
        grid=(B, n),
        in_specs=[
            pl.BlockSpec((1, T, D), row),
            pl.BlockSpec((T, HEAD_W), lambda b, i: (i, 0)),
            pl.BlockSpec((T, HEAD_W), lambda b, i: (i, 0)),
            pl.BlockSpec((HEAD_W // 2, T), lambda b, i: (0, i)),
            pl.BlockSpec((HEAD_W // 2, T), lambda b, i: (0, i)),
            wspec((D, GROUP_W)), wspec((GROUP_W, D)), wspec((D, GROUP_W)), wspec((D, GROUP_W)),
            wspec((D, GROUP_W)), wspec((D, GROUP_W)), wspec((GROUP_W, D)),
        ],
        out_specs=(
            rows,
            pl.BlockSpec((1, GROUP_W, T), lambda b, i: (b, 0, i)),
            rows, rows, rows, rows,
            pl.BlockSpec((1, HEADS, 1, HEAD_W, T), lambda b, i: (b, 0, i, 0, 0)),
        ),
        out_shape=out_shapes,
        compiler_params=pltpu.CompilerParams(
            dimension_semantics=("arbitrary", "arbitrary"),
            vmem_limit_bytes=V7X_VMEM_LIMIT),
        name="proj",
    )(x, cs, sn, cst, snt, *w)


def _retention_kernel(q_ref, kt_ref, v_ref, g_ref, gn_ref, o_ref, state_ref):
    C = RET_CHUNK

    @pl.when(pl.program_id(1) == 0)
    def _():
        state_ref[...] = jnp.zeros_like(state_ref)

    ri = lax.broadcasted_iota(jnp.int32, (C, C), 0)
    ci = lax.broadcasted_iota(jnp.int32, (C, C), 1)
    rel = (ri - ci).astype(F32)
    gn = gn_ref[...]
    for h in range(HEADS):
        log_g = math.log(1.0 - 2.0 ** (-5.0 - h))
        d_intra = jnp.where(ri >= ci, jnp.exp(log_g * jnp.maximum(rel, 0.0)), 0.0)
        q_dec = jnp.exp(log_g * (ri.astype(F32) + 1.0))
        k_dec = jnp.exp(log_g * (C - 1.0 - ci.astype(F32)))
        c_dec = math.exp(log_g * C)
        cols = slice(h * HEAD_W, (h + 1) * HEAD_W)
        state = state_ref[h]
        for c in range(RET_ROWS // C):
            rws = slice(c * C, (c + 1) * C)
            qc = q_ref[0, rws, cols]
            ktc = kt_ref[0, cols, rws]
            vc = v_ref[0, rws, cols]
            s = _dot(qc, ktc) * d_intra
            inner = _dot(s.astype(BF16), vc)
            cross = _dot((qc.astype(F32) * q_dec).astype(BF16), state.astype(BF16))
            state = state * c_dec + _dot((ktc.astype(F32) * k_dec).astype(BF16), vc)
            y = inner + cross
            mu = jnp.mean(y, axis=-1, keepdims=True)
            yc = y - mu
            var = jnp.mean(yc * yc, axis=-1, keepdims=True)
            yn = yc * lax.rsqrt(var + HEAD_NORM_EPS) * gn[:, cols]
            gate = g_ref[0, rws, cols]
            o_ref[0, rws, cols] = (yn * (gate * jax.nn.sigmoid(gate))).astype(BF16)
        state_ref[h] = state


def _retention(rq, rkt, rv, rg, gn):
    B, S, W = rq.shape
    T = RET_ROWS
    row = lambda b, i: (b, i, 0)
    rows = pl.BlockSpec((1, T, W), row)
    return pl.pallas_call(
        _retention_kernel,
        grid=(B, S // T),
        in_specs=[rows, pl.BlockSpec((1, W, T), lambda b, i: (b, 0, i)), rows, rows,
                  pl.BlockSpec((1, W), lambda b, i: (0, 0))],
        out_specs=rows,
        out_shape=jax.ShapeDtypeStruct((B, S, W), BF16),
        scratch_shapes=[pltpu.VMEM((HEADS, HEAD_W, HEAD_W), F32)],
        compiler_params=pltpu.CompilerParams(
            dimension_semantics=("arbitrary", "arbitrary"),
            vmem_limit_bytes=V7X_VMEM_LIMIT),
        name="retention",
    )(rq, rkt, rv, rg, gn)


def _diffattn_kernel(lq1_ref, lk1_ref, lq2_ref, lk2_ref, q_ref, k_ref, vt_ref, g_ref,
                     o_ref, qs_ref, sa_ref, sb_ref, m_ref, acc_ref, *, lam_init):
    TQ, TK = ATT_TQ, ATT_TK
    qi = pl.program_id(2)
    q = q_ref[0]
    lane = lax.broadcasted_iota(jnp.int32, q.shape, 1)
    zero = jnp.zeros_like(q)
    qs = jnp.concatenate([jnp.where(lane < DIFF_DQK, q, zero),
                          jnp.where(lane >= DIFF_DQK, q, zero)], axis=0)

    qs_ref[...] = qs
    m_ref[...] = jnp.full_like(m_ref, -jnp.inf)
    acc_ref[...] = jnp.zeros_like(acc_ref)
    ones_rows = jnp.where(lax.broadcasted_iota(jnp.int32, (ONES_ROWS, TK), 0) == 0,
                          1.0, 0.0).astype(BF16)

    def scores(j, s_ref):
        start = pl.multiple_of(j * TK, TK)
        kj = k_ref[0, pl.ds(start, TK), :]
        s_ref[...] = _dot_nt(kj, qs_ref[...])

    def accumulate(j, s_ref, masked):
        st = s_ref[...]
        if masked:
            kpos = j * TK + lax.broadcasted_iota(jnp.int32, st.shape, 0)
            col = lax.broadcasted_iota(jnp.int32, st.shape, 1)
            qpos = qi * TQ + jnp.where(col >= TQ, col - TQ, col)
            st = jnp.where(kpos <= qpos, st, -jnp.inf)
        m_prev = m_ref[...]
        m_new = jnp.maximum(m_prev, jnp.max(st, axis=0, keepdims=True))
        p = jnp.exp2(st - m_new).astype(BF16)
        corr = jnp.exp2(m_prev - m_new)
        vt = jnp.concatenate([vt_ref[0, 0, j], ones_rows], axis=0)
        acc_ref[...] = acc_ref[...] * corr + _dot(vt, p)
        m_ref[...] = m_new

    scores(0, sa_ref)

    def pair(jj, carry):
        j = 2 * jj
        scores(j + 1, sb_ref)
        accumulate(j, sa_ref, False)
        scores(j + 2, sa_ref)
        accumulate(j + 1, sb_ref, False)
        return carry

    lax.fori_loop(0, qi // 2, pair, 0)
    odd = (qi % 2) == 1

    @pl.when(odd)
    def _():
        scores(qi, sb_ref)
        accumulate(qi - 1, sa_ref, False)
        accumulate(qi, sb_ref, True)

    @pl.when(jnp.logical_not(odd))
    def _():
        accumulate(qi, sa_ref, True)

    lam = (jnp.exp(jnp.sum(lq1_ref[...] * lk1_ref[...], axis=-1, keepdims=True))
           - jnp.exp(jnp.sum(lq2_ref[...] * lk2_ref[...], axis=-1, keepdims=True))
           + lam_init)
    ot = acc_ref[0:HEAD_W, :] / acc_ref[HEAD_W:HEAD_W + 1, :]
    dt = ot[:, :TQ] - lam * ot[:, TQ:]
    ms = jnp.mean(dt * dt, axis=0, keepdims=True)
    dn = dt * lax.rsqrt(ms + HEAD_NORM_EPS)
    o_ref[0] = (dn.T * g_ref[...] * (1.0 - lam_init)).astype(BF16)


def _diffattn(dq, dk, dvt, lam_vecs, g, lam_init):
    B, S, W = dq.shape
    TQ = ATT_TQ
    nk = dvt.shape[2]
    vec = pl.BlockSpec((1, DIFF_DQK), lambda b, h, i: (0, 0))
    return pl.pallas_call(
        functools.partial(_diffattn_kernel, lam_init=lam_init),
        grid=(B, HEADS, S // TQ),
        in_specs=[vec, vec, vec, vec,
                  pl.BlockSpec((1, TQ, HEAD_W), lambda b, h, i: (b, i, h)),
                  pl.BlockSpec((1, S, HEAD_W), lambda b, h, i: (b, 0, h)),
                  pl.BlockSpec((1, 1, nk, HEAD_W, ATT_TK), lambda b, h, i: (b, h, 0, 0, 0)),
                  pl.BlockSpec((1, HEAD_W), lambda b, h, i: (0, h))],
        out_specs=pl.BlockSpec((1, TQ, HEAD_W), lambda b, h, i: (b, i, h)),
        out_shape=jax.ShapeDtypeStruct((B, S, W), BF16),
        scratch_shapes=[pltpu.VMEM((2 * TQ, HEAD_W), BF16),
                        pltpu.VMEM((ATT_TK, 2 * TQ), F32),
                        pltpu.VMEM((ATT_TK, 2 * TQ), F32),
                        pltpu.VMEM((1, 2 * TQ), F32),
                        pltpu.VMEM((HEAD_W + ONES_ROWS, 2 * TQ), F32)],
        compiler_params=pltpu.CompilerParams(
            dimension_semantics=("arbitrary", "arbitrary", "arbitrary"),
            vmem_limit_bytes=V7X_VMEM_LIMIT),
        name="diffattn",
    )(*lam_vecs, dq, dk, dvt, g)


def _outproj_kernel(x_ref, ret_ref, dif_ref, wr_ref, wd_ref, g_ref, b_ref, o_ref):
    mix = _dot(ret_ref[...], wr_ref[...]) + _dot(dif_ref[...], wd_ref[...])
    o_ref[...] = _layer_norm(DEEPNORM_ALPHA * x_ref[...] + mix, g_ref[...], b_ref[...])


def _outproj(x2, ret2, dif2, w_ret, w_dif, g, b):
    R, D = x2.shape
    T = OUT_ROWS
    row = lambda i: (i, 0)
    const = lambda i: (0, 0)
    return pl.pallas_call(
        _outproj_kernel,
        grid=(R // T,),
        in_specs=[pl.BlockSpec((T, D), row),
                  pl.BlockSpec((T, GROUP_W), row), pl.BlockSpec((T, GROUP_W), row),
                  pl.BlockSpec((GROUP_W, D), const), pl.BlockSpec((GROUP_W, D), const),
                  pl.BlockSpec((1, D), const), pl.BlockSpec((1, D), const)],
        out_specs=pl.BlockSpec((T, D), row),
        out_shape=jax.ShapeDtypeStruct((R, D), F32),
        compiler_params=pltpu.CompilerParams(
            dimension_semantics=("arbitrary",), vmem_limit_bytes=V7X_VMEM_LIMIT),
        name="outproj",
    )(x2, ret2, dif2, w_ret, w_dif, g, b)


def _ffn_kernel(x_ref, halo_ref, wa_ref, wb_ref, cwa_ref, cwb_ref, cba_ref, cbb_ref,
                wd_ref, g_ref, b_ref, o_ref, xb_ref, u_ref, acc_ref, *, tiles_per_seq):
    T = FFN_ROWS
    i = pl.program_id(0)
    j = pl.program_id(1)
    first_of_seq = (i % tiles_per_seq) == 0

    @pl.when(j == 0)
    def _():
        xb_ref[0:HALO, :] = halo_ref[...].astype(BF16)
        xb_ref[HALO:, :] = x_ref[...].astype(BF16)

    def conv_half(w_ref, cw_ref, cb_ref):
        u_ref[...] = _dot(xb_ref[...], w_ref[...])

        @pl.when(first_of_seq)
        def _():
            u_ref[0:HALO, :] = jnp.zeros((HALO, FFN_COLS), F32)

        cw = cw_ref[...]
        return (cw[2:3] * u_ref[HALO:HALO + T, :]
                + cw[1:2] * u_ref[HALO - 1:HALO - 1 + T, :]
                + cw[0:1] * u_ref[HALO - 2:HALO - 2 + T, :]
                + cb_ref[...])

    ya = conv_half(wa_ref, cwa_ref, cba_ref)
    yb = conv_half(wb_ref, cwb_ref, cbb_ref)
    gated = (ya * jax.nn.sigmoid(ya) * yb).astype(BF16)
    part = _dot(gated, wd_ref[...])

    @pl.when(j == 0)
    def _():
        acc_ref[...] = part

    @pl.when(j > 0)
    def _():
        acc_ref[...] += part

    @pl.when(j == pl.num_programs(1) - 1)
    def _():
        o_ref[...] = _layer_norm(DEEPNORM_ALPHA * x_ref[...] + acc_ref[...],
                                 g_ref[...], b_ref[...])


def _ffn(x2, w_a, w_b, cw_a, cw_b, cb_a, cb_b, w_down, g, b, seq):
    R, D = x2.shape
    T = FFN_ROWS
    C = FFN_COLS
    tiles_per_seq = seq // T
    const = lambda i, j: (0, 0)
    col = lambda i, j: (0, j)
    return pl.pallas_call(
        functools.partial(_ffn_kernel, tiles_per_seq=tiles_per_seq),
        grid=(R // T, D_FF // C),
        in_specs=[pl.BlockSpec((T, D), lambda i, j: (i, 0)),
                  pl.BlockSpec((HALO, D), lambda i, j: (jnp.maximum(i * (T // HALO) - 1, 0), 0)),
                  pl.BlockSpec((D, C), col), pl.BlockSpec((D, C), col),
                  pl.BlockSpec((3, C), col), pl.BlockSpec((3, C), col),
                  pl.BlockSpec((1, C), col), pl.BlockSpec((1, C), col),
                  pl.BlockSpec((C, D), lambda i, j: (j, 0)),
                  pl.BlockSpec((1, D), const), pl.BlockSpec((1, D), const)],
        out_specs=pl.BlockSpec((T, D), lambda i, j: (i, 0)),
        out_shape=jax.ShapeDtypeStruct((R, D), F32),
        scratch_shapes=[pltpu.VMEM((HALO + T, D), BF16),
                        pltpu.VMEM((HALO + T, C), F32),
                        pltpu.VMEM((T, D), F32)],
        compiler_params=pltpu.CompilerParams(
            dimension_semantics=("arbitrary", "arbitrary"),
            vmem_limit_bytes=V7X_VMEM_LIMIT),
        name="ffn",
    )(x2, x2, w_a, w_b, cw_a, cw_b, cb_a, cb_b, w_down, g, b)


def _rotary_tables(seq):
    half = HEAD_W // 2
    inv_freq = ROPE_BASE ** (-jnp.arange(half, dtype=F32) / half)
    ang = jnp.arange(seq).astype(F32)[:, None] * inv_freq[None, :]
    cos, sin = jnp.cos(ang), jnp.sin(ang)
    return (jnp.concatenate([cos, cos], axis=1), jnp.concatenate([-sin, sin], axis=1),
            cos.T, sin.T)


def kernel(x, w_in, ret_gn_g, diff_norm_g, lam_q1, lam_k1, lam_q2, lam_k2, w_out,
           ln1_g, ln1_b, w_up, conv_w, conv_b, w_down, ln2_g, ln2_b):
    B, S, D = x.shape
    tabs = _rotary_tables(S)
    G = GROUP_W
    for l in range(DEPTH):
        lam_init = 0.8 - 0.6 * math.exp(-0.3 * l)
        wl = w_in[l].astype(BF16)
        w = (wl[:, 0:G], wl[:, G:2 * G].T, wl[:, 2 * G:3 * G], wl[:, 3 * G:4 * G],
             wl[:, 4 * G:5 * G], wl[:, 5 * G:6 * G], wl[:, 6 * G:7 * G].T)
        rq, rkt, rv, rg, dq, dk, dvt = _proj(x, tabs, w)
        ret = _retention(rq, rkt, rv, rg, ret_gn_g[l][None, :])
        lam_vecs = (lam_q1[l][None, :], lam_k1[l][None, :], lam_q2[l][None, :], lam_k2[l][None, :])
        dif = _diffattn(dq, dk, dvt, lam_vecs, diff_norm_g[l][None, :], lam_init)
        wo = w_out[l].astype(BF16)
        x2 = _outproj(x.reshape(B * S, D), ret.reshape(B * S, G), dif.reshape(B * S, G),
                      wo[:G], wo[G:], ln1_g[l][None, :], ln1_b[l][None, :])
        wu = w_up[l].astype(BF16)
        x2 = _ffn(x2, wu[:, :D_FF], wu[:, D_FF:], conv_w[l][:, :D_FF], conv_w[l][:, D_FF:],
                  conv_b[l][None, :D_FF], conv_b[l][None, D_FF:], w_down[l].astype(BF16),
                  ln2_g[l][None, :], ln2_b[l][None, :], S)
        x = x2.reshape(B, S, D)
    return x
```

```python
import functools
import math

import jax
import jax.numpy as jnp
from jax import lax
from jax.experimental import pallas as pl
from jax.experimental.pallas import tpu as pltpu

F32 = jnp.float32
BF16 = jnp.bfloat16

D_MODEL = 1024
DEPTH = 2
HEADS = 4
HEAD_W = 128
GROUP_W = HEADS * HEAD_W
RET_CHUNK = 128
DIFF_DQK = 64
D_FF = 2816
ROPE_BASE = 10000.0
LN_EPS = 1e-5
HEAD_NORM_EPS = 1e-5
DEEPNORM_ALPHA = (2.0 * DEPTH) ** 0.25

V7X_VMEM_LIMIT = 56 * 1024 * 1024

PROJ_ROWS = 512
RET_ROWS = 512
ATT_TK = PROJ_ROWS
ATT_TQ = ATT_TK
FFN_ROWS = 512
FFN_COLS = 512
CARRY_ROWS = 8
ONES_ROWS = 16
DIFF_Q_SCALE = DIFF_DQK ** -0.5 * math.log2(math.e)

NT_DIMS = (((1,), (1,)), ((), ()))


def _dot(a, b):
    return jnp.dot(a, b, preferred_element_type=F32)


def _dot_nt(a, b):
    return lax.dot_general(a, b, NT_DIMS, preferred_element_type=F32)


def _layer_norm(y, g, b):
    mu = jnp.mean(y, axis=-1, keepdims=True)
    yc = y - mu
    var = jnp.mean(yc * yc, axis=-1, keepdims=True)
    return yc * lax.rsqrt(var + LN_EPS) * g + b


def _proj_kernel(x_ref, cs_ref, sn_ref, cst_ref, snt_ref,
                 wrq_ref, wrkt_ref, wrv_ref, wrg_ref, wdq_ref, wdk_ref, wdvt_ref,
                 rq_ref, rkt_ref, rv_ref, rg_ref, dq_ref, dk_ref, dvt_ref):
    xb = x_ref[0].astype(BF16)
    t = _dot(xb, wrq_ref[...])
    cs = cs_ref[...]
    sn = sn_ref[...]
    for h in range(HEADS):
        th = t[:, h * HEAD_W:(h + 1) * HEAD_W]
        rq_ref[0, :, h * HEAD_W:(h + 1) * HEAD_W] = (
            th * cs + pltpu.roll(th, HEAD_W // 2, 1) * sn).astype(BF16)
    tt = _dot_nt(wrkt_ref[...], xb)
    cst = cst_ref[...]
    snt = snt_ref[...]
    half = HEAD_W // 2
    kscale = HEAD_W ** -0.5
    for h in range(HEADS):
        a = tt[h * HEAD_W:h * HEAD_W + half]
        b = tt[h * HEAD_W + half:(h + 1) * HEAD_W]
        rkt_ref[0, h * HEAD_W:h * HEAD_W + half, :] = ((a * cst - b * snt) * kscale).astype(BF16)
        rkt_ref[0, h * HEAD_W + half:(h + 1) * HEAD_W, :] = ((a * snt + b * cst) * kscale).astype(BF16)
    rv_ref[0] = _dot(xb, wrv_ref[...]).astype(BF16)
    rg_ref[0] = _dot(xb, wrg_ref[...])
    dq_ref[0] = (_dot(xb, wdq_ref[...]) * DIFF_Q_SCALE).astype(BF16)
    dk_ref[0] = _dot(xb, wdk_ref[...]).astype(BF16)
    dvt = _dot_nt(wdvt_ref[...], xb).astype(BF16)
    dvt_ref[0, :, 0] = dvt.reshape(HEADS, HEAD_W, PROJ_ROWS)


def _proj(x, tabs, w):
    B, S, D = x.shape
    T = PROJ_ROWS
    n = S // T
    cs, sn, cst, snt = tabs
    row = lambda b, i: (b, i, 0)
    const2 = lambda b, i: (0, 0)
    wspec = lambda shape: pl.BlockSpec(shape, const2)
    out_shapes = (
        jax.ShapeDtypeStruct((B, S, GROUP_W), BF16),
        jax.ShapeDtypeStruct((B, GROUP_W, S), BF16),
        jax.ShapeDtypeStruct((B, S, GROUP_W), BF16),
        jax.ShapeDtypeStruct((B, S, GROUP_W), F32),
        jax.ShapeDtypeStruct((B, S, GROUP_W), BF16),
        jax.ShapeDtypeStruct((B, S, GROUP_W), BF16),
        jax.ShapeDtypeStruct((B, HEADS, n, HEAD_W, T), BF16),
    )
    rows = pl.BlockSpec((1, T, GROUP_W), row)
    return pl.pallas_call(
        _proj_kernel,
        grid=(B, n),
        in_specs=[
            pl.BlockSpec((1, T, D), row),
            pl.BlockSpec((T, HEAD_W), lambda b, i: (i, 0)),
            pl.BlockSpec((T, HEAD_W), lambda b, i: (i, 0)),
            pl.BlockSpec((HEAD_W // 2, T), lambda b, i: (0, i)),
            pl.BlockSpec((HEAD_W // 2, T), lambda b, i: (0, i)),
            wspec((D, GROUP_W)), wspec((GROUP_W, D)), wspec((D, GROUP_W)), wspec((D, GROUP_W)),
            wspec((D, GROUP_W)), wspec((D, GROUP_W)), wspec((GROUP_W, D)),
        ],
        out_specs=(
            rows,
            pl.BlockSpec((1, GROUP_W, T), lambda b, i: (b, 0, i)),
            rows, rows, rows, rows,
            pl.BlockSpec((1, HEADS, 1, HEAD_W, T), lambda b, i: (b, 0, i, 0, 0)),
        ),
        out_shape=out_shapes,
        compiler_params=pltpu.CompilerParams(
            dimension_semantics=("arbitrary", "arbitrary"),
            vmem_limit_bytes=V7X_VMEM_LIMIT),
        name="proj",
    )(x, cs, sn, cst, snt, *w)


def _retention_kernel(q_ref, kt_ref, v_ref, g_ref, gn_ref, o_ref, state_ref):
    C = RET_CHUNK

    @pl.when(pl.program_id(1) == 0)
    def _():
        state_ref[...] = jnp.zeros_like(state_ref)

    ri = lax.broadcasted_iota(jnp.int32, (C, C), 0)
    ci = lax.broadcasted_iota(jnp.int32, (C, C), 1)
    rel = (ri - ci).astype(F32)
    gn = gn_ref[...]
    for h in range(HEADS):
        log_g = math.log(1.0 - 2.0 ** (-5.0 - h))
        d_intra = jnp.where(ri >= ci, jnp.exp(log_g * jnp.maximum(rel, 0.0)), 0.0)
        q_dec = jnp.exp(log_g * (ri.astype(F32) + 1.0))
        k_dec = jnp.exp(log_g * (C - 1.0 - ci.astype(F32)))
        c_dec = math.exp(log_g * C)
        cols = slice(h * HEAD_W, (h + 1) * HEAD_W)
        state = state_ref[h]
        for c in range(RET_ROWS // C):
            rws = slice(c * C, (c + 1) * C)
            qc = q_ref[0, rws, cols]
            ktc = kt_ref[0, cols, rws]
            vc = v_ref[0, rws, cols]
            s = _dot(qc, ktc) * d_intra
            inner = _dot(s.astype(BF16), vc)
            cross = _dot((qc.astype(F32) * q_dec).astype(BF16), state.astype(BF16))
            state = state * c_dec + _dot((ktc.astype(F32) * k_dec).astype(BF16), vc)
            y = inner + cross
            mu = jnp.mean(y, axis=-1, keepdims=True)
            yc = y - mu
            var = jnp.mean(yc * yc, axis=-1, keepdims=True)
            yn = yc * lax.rsqrt(var + HEAD_NORM_EPS) * gn[:, cols]
            gate = g_ref[0, rws, cols]
            o_ref[0, rws, cols] = (yn * (gate * jax.nn.sigmoid(gate))).astype(BF16)
        state_ref[h] = state


def _retention(rq, rkt, rv, rg, gn):
    B, S, W = rq.shape
    T = RET_ROWS
    row = lambda b, i: (b, i, 0)
    rows = pl.BlockSpec((1, T, W), row)
    return pl.pallas_call(
        _retention_kernel,
        grid=(B, S // T),
        in_specs=[rows, pl.BlockSpec((1, W, T), lambda b, i: (b, 0, i)), rows, rows,
                  pl.BlockSpec((1, W), lambda b, i: (0, 0))],
        out_specs=rows,
        out_shape=jax.ShapeDtypeStruct((B, S, W), BF16),
        scratch_shapes=[pltpu.VMEM((HEADS, HEAD_W, HEAD_W), F32)],
        compiler_params=pltpu.CompilerParams(
            dimension_semantics=("arbitrary", "arbitrary"),
            vmem_limit_bytes=V7X_VMEM_LIMIT),
        name="retention",
    )(rq, rkt, rv, rg, gn)


def _diffattn_kernel(lq1_ref, lk1_ref, lq2_ref, lk2_ref, q_ref, k_ref, vt_ref, g_ref,
                     o_ref, qs_ref, sa_ref, sb_ref, m_ref, acc_ref, *, lam_init):
    TQ, TK = ATT_TQ, ATT_TK
    qi = pl.program_id(2)
    qt = q_ref[0].astype(F32).T
    chan = lax.broadcasted_iota(jnp.int32, qt.shape, 0)
    qs_ref[...] = jnp.concatenate([jnp.where(chan < DIFF_DQK, qt, 0.0),
                                   jnp.where(chan >= DIFF_DQK, qt, 0.0)],
                                  axis=1).astype(BF16)
    m_ref[...] = jnp.full_like(m_ref, -jnp.inf)
    acc_ref[...] = jnp.zeros_like(acc_ref)
    ones_rows = jnp.where(lax.broadcasted_iota(jnp.int32, (ONES_ROWS, TK), 0) == 0,
                          1.0, 0.0).astype(BF16)

    def scores(j, s_ref):
        start = pl.multiple_of(j * TK, TK)
        kj = k_ref[0, pl.ds(start, TK), :]
        s_ref[...] = _dot(kj, qs_ref[...])

    def accumulate(j, s_ref, masked):
        st = s_ref[...]
        if masked:
            kpos = j * TK + lax.broadcasted_iota(jnp.int32, st.shape, 0)
            col = lax.broadcasted_iota(jnp.int32, st.shape, 1)
            qpos = qi * TQ + jnp.where(col >= TQ, col - TQ, col)
            st = jnp.where(kpos <= qpos, st, -jnp.inf)
        m_prev = m_ref[...]
        m_new = jnp.maximum(m_prev, jnp.max(st, axis=0, keepdims=True))
        p = jnp.exp2(st - m_new).astype(BF16)
        corr = jnp.exp2(m_prev - m_new)
        vt = jnp.concatenate([vt_ref[0, 0, j], ones_rows], axis=0)
        acc_ref[...] = acc_ref[...] * corr + _dot(vt, p)
        m_ref[...] = m_new

    scores(0, sa_ref)

    def pair(jj, carry):
        j = 2 * jj
        scores(j + 1, sb_ref)
        accumulate(j, sa_ref, False)
        scores(j + 2, sa_ref)
        accumulate(j + 1, sb_ref, False)
        return carry

    lax.fori_loop(0, qi // 2, pair, 0)
    odd = (qi % 2) == 1

    @pl.when(odd)
    def _():
        scores(qi, sb_ref)
        accumulate(qi - 1, sa_ref, False)
        accumulate(qi, sb_ref, True)

    @pl.when(jnp.logical_not(odd))
    def _():
        accumulate(qi, sa_ref, True)

    lam = (jnp.exp(jnp.sum(lq1_ref[...] * lk1_ref[...], axis=-1, keepdims=True))
           - jnp.exp(jnp.sum(lq2_ref[...] * lk2_ref[...], axis=-1, keepdims=True))
           + lam_init)
    ot = acc_ref[0:HEAD_W, :] / acc_ref[HEAD_W:HEAD_W + 1, :]
    dt = ot[:, :TQ] - lam * ot[:, TQ:]
    ms = jnp.mean(dt * dt, axis=0, keepdims=True)
    dn = dt * lax.rsqrt(ms + HEAD_NORM_EPS)
    o_ref[0] = (dn.T * g_ref[...] * (1.0 - lam_init)).astype(BF16)


def _diffattn(dq, dk, dvt, lam_vecs, g, lam_init):
    B, S, W = dq.shape
    TQ = ATT_TQ
    nk = dvt.shape[2]
    vec = pl.BlockSpec((1, DIFF_DQK), lambda b, h, i: (0, 0))
    return pl.pallas_call(
        functools.partial(_diffattn_kernel, lam_init=lam_init),
        grid=(B, HEADS, S // TQ),
        in_specs=[vec, vec, vec, vec,
                  pl.BlockSpec((1, TQ, HEAD_W), lambda b, h, i: (b, i, h)),
                  pl.BlockSpec((1, S, HEAD_W), lambda b, h, i: (b, 0, h)),
                  pl.BlockSpec((1, 1, nk, HEAD_W, ATT_TK), lambda b, h, i: (b, h, 0, 0, 0)),
                  pl.BlockSpec((1, HEAD_W), lambda b, h, i: (0, h))],
        out_specs=pl.BlockSpec((1, TQ, HEAD_W), lambda b, h, i: (b, i, h)),
        out_shape=jax.ShapeDtypeStruct((B, S, W), BF16),
        scratch_shapes=[pltpu.VMEM((HEAD_W, 2 * TQ), BF16),
                        pltpu.VMEM((ATT_TK, 2 * TQ), F32),
                        pltpu.VMEM((ATT_TK, 2 * TQ), F32),
                        pltpu.VMEM((1, 2 * TQ), F32),
                        pltpu.VMEM((HEAD_W + ONES_ROWS, 2 * TQ), F32)],
        compiler_params=pltpu.CompilerParams(
            dimension_semantics=("arbitrary", "arbitrary", "arbitrary"),
            vmem_limit_bytes=V7X_VMEM_LIMIT),
        name="diffattn",
    )(*lam_vecs, dq, dk, dvt, g)


def _mixffn_kernel(x_ref, ret_ref, dif_ref, wor_ref, wod_ref, g1_ref, b1_ref,
                   wa_ref, wb_ref, cwa_ref, cwb_ref, cba_ref, cbb_ref, wd_ref, g2_ref, b2_ref,
                   o_ref, x1_ref, carry_ref, ua0_ref, ub0_ref, ua1_ref, ub1_ref, g0_ref, g1s_ref,
                   acc_ref, *, tiles_per_seq):
    T = FFN_ROWS
    C = FFN_COLS
    u_refs = ((ua0_ref, ub0_ref), (ua1_ref, ub1_ref))
    g_refs = (g0_ref, g1s_ref)
    first_of_seq = (pl.program_id(0) % tiles_per_seq) == 0

    @pl.when(first_of_seq)
    def _():
        carry_ref[...] = jnp.zeros_like(carry_ref)

    mix = _dot(ret_ref[...], wor_ref[...]) + _dot(dif_ref[...], wod_ref[...])
    x1 = _layer_norm(DEEPNORM_ALPHA * x_ref[...] + mix, g1_ref[...], b1_ref[...])
    x1_ref[...] = x1
    xb = x1.astype(BF16)

    starts = list(range(0, D_FF, C))
    widths = [min(C, D_FF - s0) for s0 in starts]
    n_tiles = len(starts)

    def up(t):
        s, w = t % 2, widths[t]
        cols = slice(starts[t], starts[t] + w)
        for half, w_ref in enumerate((wa_ref, wb_ref)):
            ccols = slice(half * D_FF + starts[t], half * D_FF + starts[t] + w)
            u = _dot(xb, w_ref[:, cols])
            u_refs[s][half][0:CARRY_ROWS, 0:w] = carry_ref[:, ccols]
            u_refs[s][half][CARRY_ROWS:, 0:w] = u
            carry_ref[:, ccols] = u[T - CARRY_ROWS:, :]

    def conv_half(t, half, cw_ref, cb_ref):
        w = widths[t]
        cols = slice(starts[t], starts[t] + w)
        ue = u_refs[t % 2][half][:, 0:w]
        cw = cw_ref[:, cols]
        y = cw[2:3] * ue + cw[1:2] * pltpu.roll(ue, 1, 0) + cw[0:1] * pltpu.roll(ue, 2, 0)
        return y[CARRY_ROWS:, :] + cb_ref[:, cols]

    def gate(t):
        ya = conv_half(t, 0, cwa_ref, cba_ref)
        yb = conv_half(t, 1, cwb_ref, cbb_ref)
        g_refs[t % 2][:, 0:widths[t]] = (ya * jax.nn.sigmoid(ya) * yb).astype(BF16)

    def down(t):
        w = widths[t]
        part = _dot(g_refs[t % 2][:, 0:w], wd_ref[starts[t]:starts[t] + w, :])
        if t == 0:
            acc_ref[...] = part
        else:
            acc_ref[...] += part

    up(0)
    for t in range(n_tiles + 1):
        if t + 1 < n_tiles:
            up(t + 1)
        if t >= 1:
            down(t - 1)
        if t < n_tiles:
            gate(t)

    o_ref[...] = _layer_norm(DEEPNORM_ALPHA * x1_ref[...] + acc_ref[...],
                             g2_ref[...], b2_ref[...])


def _mixffn(x2, ret2, dif2, w_or, w_od, g1, b1, w_a, w_b, cw_a, cw_b, cb_a, cb_b, w_down,
            g2, b2, seq):
    R, D = x2.shape
    T = FFN_ROWS
    row = lambda i: (i, 0)
    resident = lambda shape: pl.BlockSpec(shape, lambda i: (0, 0), pipeline_mode=pl.Buffered(1))
    return pl.pallas_call(
        functools.partial(_mixffn_kernel, tiles_per_seq=seq // T),
        grid=(R // T,),
        in_specs=[pl.BlockSpec((T, D), row),
                  pl.BlockSpec((T, GROUP_W), row), pl.BlockSpec((T, GROUP_W), row),
                  resident((GROUP_W, D)), resident((GROUP_W, D)),
                  resident((1, D)), resident((1, D)),
                  resident((D, D_FF)), resident((D, D_FF)),
                  resident((3, D_FF)), resident((3, D_FF)),
                  resident((1, D_FF)), resident((1, D_FF)),
                  resident((D_FF, D)),
                  resident((1, D)), resident((1, D))],
        out_specs=pl.BlockSpec((T, D), row),
        out_shape=jax.ShapeDtypeStruct((R, D), F32),
        scratch_shapes=[pltpu.VMEM((T, D), F32),
                        pltpu.VMEM((CARRY_ROWS, 2 * D_FF), F32),
                        *[pltpu.VMEM((CARRY_ROWS + T, FFN_COLS), F32)] * 4,
                        *[pltpu.VMEM((T, FFN_COLS), BF16)] * 2,
                        pltpu.VMEM((T, D), F32)],
        compiler_params=pltpu.CompilerParams(
            dimension_semantics=("arbitrary",), vmem_limit_bytes=V7X_VMEM_LIMIT),
        name="mixffn",
    )(x2, ret2, dif2, w_or, w_od, g1, b1, w_a, w_b, cw_a, cw_b, cb_a, cb_b, w_down, g2, b2)


def _rotary_tables(seq):
    half = HEAD_W // 2
    inv_freq = ROPE_BASE ** (-jnp.arange(half, dtype=F32) / half)
    ang = jnp.arange(seq).astype(F32)[:, None] * inv_freq[None, :]
    cos, sin = jnp.cos(ang), jnp.sin(ang)
    return (jnp.concatenate([cos, cos], axis=1), jnp.concatenate([-sin, sin], axis=1),
            cos.T, sin.T)


def kernel(x, w_in, ret_gn_g, diff_norm_g, lam_q1, lam_k1, lam_q2, lam_k2, w_out,
           ln1_g, ln1_b, w_up, conv_w, conv_b, w_down, ln2_g, ln2_b):
    B, S, D = x.shape
    tabs = _rotary_tables(S)
    G = GROUP_W
    for l in range(DEPTH):
        lam_init = 0.8 - 0.6 * math.exp(-0.3 * l)
        wl = w_in[l].astype(BF16)
        w = (wl[:, 0:G], wl[:, G:2 * G].T, wl[:, 2 * G:3 * G], wl[:, 3 * G:4 * G],
             wl[:, 4 * G:5 * G], wl[:, 5 * G:6 * G], wl[:, 6 * G:7 * G].T)
        rq, rkt, rv, rg, dq, dk, dvt = _proj(x, tabs, w)
        ret = _retention(rq, rkt, rv, rg, ret_gn_g[l][None, :])
        lam_vecs = (lam_q1[l][None, :], lam_k1[l][None, :], lam_q2[l][None, :], lam_k2[l][None, :])
        dif = _diffattn(dq, dk, dvt, lam_vecs, diff_norm_g[l][None, :], lam_init)
        wo = w_out[l].astype(BF16)
        wu = w_up[l].astype(BF16)
        x2 = _mixffn(x.reshape(B * S, D), ret.reshape(B * S, G), dif.reshape(B * S, G),
                     wo[:G], wo[G:], ln1_g[l][None, :], ln1_b[l][None, :],
                     wu[:, :D_FF], wu[:, D_FF:], conv_w[l][:, :D_FF], conv_w[l][:, D_FF:],
                     conv_b[l][None, :D_FF], conv_b[l][None, D_FF:], w_down[l].astype(BF16),
                     ln2_g[l][None, :], ln2_b[l][None, :], S)
        x = x2.reshape(B, S, D)
    return x
```

```python
import functools
import math

import jax
import jax.numpy as jnp
from jax import lax
from jax.experimental import pallas as pl
from jax.experimental.pallas import tpu as pltpu

F32 = jnp.float32
BF16 = jnp.bfloat16

D_MODEL = 1024
DEPTH = 2
HEADS = 4
HEAD_W = 128
GROUP_W = HEADS * HEAD_W
RET_CHUNK = 128
DIFF_DQK = 64
D_FF = 2816
ROPE_BASE = 10000.0
LN_EPS = 1e-5
HEAD_NORM_EPS = 1e-5
DEEPNORM_ALPHA = (2.0 * DEPTH) ** 0.25

V7X_VMEM_LIMIT = 56 * 1024 * 1024

PROJ_ROWS = 512
RET_ROWS = 512
ATT_TK = PROJ_ROWS
ATT_TQ = ATT_TK
FFN_ROWS = 512
FFN_COLS = 512
CARRY_ROWS = 8
ONES_ROWS = 16
DIFF_Q_SCALE = DIFF_DQK ** -0.5 * math.log2(math.e)

NT_DIMS = (((1,), (1,)), ((), ()))


def _dot(a, b):
    return jnp.dot(a, b, preferred_element_type=F32)


def _dot_nt(a, b):
    return lax.dot_general(a, b, NT_DIMS, preferred_element_type=F32)


def _layer_norm(y, g, b):
    mu = jnp.mean(y, axis=-1, keepdims=True)
    yc = y - mu
    var = jnp.mean(yc * yc, axis=-1, keepdims=True)
    return yc * lax.rsqrt(var + LN_EPS) * g + b


def _proj_kernel(x_ref, cs_ref, sn_ref, cst_ref, snt_ref,
                 wrq_ref, wrkt_ref, wrv_ref, wrg_ref, wdq_ref, wdk_ref, wdvt_ref,
                 rq_ref, rkt_ref, rv_ref, rg_ref, dq_ref, dk_ref, dvt_ref):
    xb = x_ref[0].astype(BF16)
    t = _dot(xb, wrq_ref[...])
    cs = cs_ref[...]
    sn = sn_ref[...]
    for h in range(HEADS):
        th = t[:, h * HEAD_W:(h + 1) * HEAD_W]
        rq_ref[0, :, h * HEAD_W:(h + 1) * HEAD_W] = (
            th * cs + pltpu.roll(th, HEAD_W // 2, 1) * sn).astype(BF16)
    tt = _dot_nt(wrkt_ref[...], xb)
    cst = cst_ref[...]
    snt = snt_ref[...]
    half = HEAD_W // 2
    kscale = HEAD_W ** -0.5
    for h in range(HEADS):
        a = tt[h * HEAD_W:h * HEAD_W + half]
        b = tt[h * HEAD_W + half:(h + 1) * HEAD_W]
        rkt_ref[0, h * HEAD_W:h * HEAD_W + half, :] = ((a * cst - b * snt) * kscale).astype(BF16)
        rkt_ref[0, h * HEAD_W + half:(h + 1) * HEAD_W, :] = ((a * snt + b * cst) * kscale).astype(BF16)
    rv_ref[0] = _dot(xb, wrv_ref[...]).astype(BF16)
    rg_ref[0] = _dot(xb, wrg_ref[...])
    dq_ref[0] = (_dot(xb, wdq_ref[...]) * DIFF_Q_SCALE).astype(BF16)
    dk_ref[0] = _dot(xb, wdk_ref[...]).astype(BF16)
    dvt = _dot_nt(wdvt_ref[...], xb).astype(BF16)
    dvt_ref[0, :, 0] = dvt.reshape(HEADS, HEAD_W, PROJ_ROWS)


def _proj(x, tabs, w):
    B, S, D = x.shape
    T = PROJ_ROWS
    n = S // T
    cs, sn, cst, snt = tabs
    row = lambda b, i: (b, i, 0)
    const2 = lambda b, i: (0, 0)
    wspec = lambda shape: pl.BlockSpec(shape, const2)
    out_shapes = (
        jax.ShapeDtypeStruct((B, S, GROUP_W), BF16),
        jax.ShapeDtypeStruct((B, GROUP_W, S), BF16),
        jax.ShapeDtypeStruct((B, S, GROUP_W), BF16),
        jax.ShapeDtypeStruct((B, S, GROUP_W), F32),
        jax.ShapeDtypeStruct((B, S, GROUP_W), BF16),
        jax.ShapeDtypeStruct((B, S, GROUP_W), BF16),
        jax.ShapeDtypeStruct((B, HEADS, n, HEAD_W, T), BF16),
    )
    rows = pl.BlockSpec((1, T, GROUP_W), row)
    return pl.pallas_call(
        _proj_kernel,
        grid=(B, n),
        in_specs=[
            pl.BlockSpec((1, T, D), row),
            pl.BlockSpec((T, HEAD_W), lambda b, i: (i, 0)),
            pl.BlockSpec((T, HEAD_W), lambda b, i: (i, 0)),
            pl.BlockSpec((HEAD_W // 2, T), lambda b, i: (0, i)),
            pl.BlockSpec((HEAD_W // 2, T), lambda b, i: (0, i)),
            wspec((D, GROUP_W)), wspec((GROUP_W, D)), wspec((D, GROUP_W)), wspec((D, GROUP_W)),
            wspec((D, GROUP_W)), wspec((D, GROUP_W)), wspec((GROUP_W, D)),
        ],
        out_specs=(
            rows,
            pl.BlockSpec((1, GROUP_W, T), lambda b, i: (b, 0, i)),
            rows, rows, rows, rows,
            pl.BlockSpec((1, HEADS, 1, HEAD_W, T), lambda b, i: (b, 0, i, 0, 0)),
        ),
        out_shape=out_shapes,
        compiler_params=pltpu.CompilerParams(
            dimension_semantics=("arbitrary", "arbitrary"),
            vmem_limit_bytes=V7X_VMEM_LIMIT),
        name="proj",
    )(x, cs, sn, cst, snt, *w)


def _retention_kernel(q_ref, kt_ref, v_ref, g_ref, gn_ref, o_ref, state_ref):
    C = RET_CHUNK

    @pl.when(pl.program_id(1) == 0)
    def _():
        state_ref[...] = jnp.zeros_like(state_ref)

    ri = lax.broadcasted_iota(jnp.int32, (C, C), 0)
    ci = lax.broadcasted_iota(jnp.int32, (C, C), 1)
    rel = (ri - ci).astype(F32)
    gn = gn_ref[...]
    for h in range(HEADS):
        log_g = math.log(1.0 - 2.0 ** (-5.0 - h))
        d_intra = jnp.where(ri >= ci, jnp.exp(log_g * jnp.maximum(rel, 0.0)), 0.0)
        q_dec = jnp.exp(log_g * (ri.astype(F32) + 1.0))
        k_dec = jnp.exp(log_g * (C - 1.0 - ci.astype(F32)))
        c_dec = math.exp(log_g * C)
        cols = slice(h * HEAD_W, (h + 1) * HEAD_W)
        state = state_ref[h]
        for c in range(RET_ROWS // C):
            rws = slice(c * C, (c + 1) * C)
            qc = q_ref[0, rws, cols]
            ktc = kt_ref[0, cols, rws]
            vc = v_ref[0, rws, cols]
            s = _dot(qc, ktc) * d_intra
            inner = _dot(s.astype(BF16), vc)
            cross = _dot((qc.astype(F32) * q_dec).astype(BF16), state.astype(BF16))
            state = state * c_dec + _dot((ktc.astype(F32) * k_dec).astype(BF16), vc)
            y = inner + cross
            mu = jnp.mean(y, axis=-1, keepdims=True)
            yc = y - mu
            var = jnp.mean(yc * yc, axis=-1, keepdims=True)
            yn = yc * lax.rsqrt(var + HEAD_NORM_EPS) * gn[:, cols]
            gate = g_ref[0, rws, cols]
            o_ref[0, rws, cols] = (yn * (gate * jax.nn.sigmoid(gate))).astype(BF16)
        state_ref[h] = state


def _retention(rq, rkt, rv, rg, gn):
    B, S, W = rq.shape
    T = RET_ROWS
    row = lambda b, i: (b, i, 0)
    rows = pl.BlockSpec((1, T, W), row)
    return pl.pallas_call(
        _retention_kernel,
        grid=(B, S // T),
        in_specs=[rows, pl.BlockSpec((1, W, T), lambda b, i: (b, 0, i)), rows, rows,
                  pl.BlockSpec((1, W), lambda b, i: (0, 0))],
        out_specs=rows,
        out_shape=jax.ShapeDtypeStruct((B, S, W), BF16),
        scratch_shapes=[pltpu.VMEM((HEADS, HEAD_W, HEAD_W), F32)],
        compiler_params=pltpu.CompilerParams(
            dimension_semantics=("arbitrary", "arbitrary"),
            vmem_limit_bytes=V7X_VMEM_LIMIT),
        name="retention",
    )(rq, rkt, rv, rg, gn)


def _diffattn_kernel(lq1_ref, lk1_ref, lq2_ref, lk2_ref, q_ref, qn_ref, k_ref, vt_ref, g_ref,
                     o_ref, qs_ref, sa_ref, sb_ref, sc_ref, m_ref, acc_ref, *, lam_init):
    TQ, TK = ATT_TQ, ATT_TK
    qi = pl.program_id(2)

    def load_queries(src_ref):
        qt = src_ref[0].astype(F32).T
        chan = lax.broadcasted_iota(jnp.int32, qt.shape, 0)
        qs_ref[...] = jnp.concatenate([jnp.where(chan < DIFF_DQK, qt, 0.0),
                                       jnp.where(chan >= DIFF_DQK, qt, 0.0)],
                                      axis=1).astype(BF16)

    m_ref[...] = jnp.full_like(m_ref, -jnp.inf)
    acc_ref[...] = jnp.zeros_like(acc_ref)
    ones_rows = jnp.where(lax.broadcasted_iota(jnp.int32, (ONES_ROWS, TK), 0) == 0,
                          1.0, 0.0).astype(BF16)

    def scores(j, s_ref):
        start = pl.multiple_of(j * TK, TK)
        kj = k_ref[0, pl.ds(start, TK), :]
        s_ref[...] = _dot(kj, qs_ref[...])

    def accumulate(j, s_ref, masked):
        st = s_ref[...]
        if masked:
            kpos = j * TK + lax.broadcasted_iota(jnp.int32, st.shape, 0)
            col = lax.broadcasted_iota(jnp.int32, st.shape, 1)
            qpos = qi * TQ + jnp.where(col >= TQ, col - TQ, col)
            st = jnp.where(kpos <= qpos, st, -jnp.inf)
        m_prev = m_ref[...]
        m_new = jnp.maximum(m_prev, jnp.max(st, axis=0, keepdims=True))
        p = jnp.exp2(st - m_new).astype(BF16)
        corr = jnp.exp2(m_prev - m_new)
        vt = jnp.concatenate([vt_ref[0, 0, j], ones_rows], axis=0)
        acc_ref[...] = acc_ref[...] * corr + _dot(vt, p)
        m_ref[...] = m_new

    def prefetch_next_step():
        load_queries(qn_ref)
        scores(0, sc_ref)

    @pl.when(qi == 0)
    def _():
        load_queries(q_ref)
        scores(0, sc_ref)
        accumulate(0, sc_ref, True)
        prefetch_next_step()

    @pl.when(qi > 0)
    def _():
        ring = (sb_ref, sa_ref)
        scores(1, sb_ref)
        accumulate(0, sc_ref, False)
        rest = qi - 1

        def run(b, n):
            for i in range(n):
                scores(b + i + 1, ring[(i + 1) % 2])
                accumulate(b + i, ring[i % 2], False)

        def octet(t, carry):
            run(1 + 8 * t, 8)
            return carry

        lax.fori_loop(0, rest // 8, octet, 0)

        @pl.when(rest % 8 >= 4)
        def _():
            run(1 + 8 * (rest // 8), 4)

        b0 = 1 + 4 * (rest // 4)
        for r in range(4):
            @pl.when(rest % 4 == r)
            def _(r=r):
                for i in range(r + 1):
                    if i < r:
                        scores(b0 + i + 1, ring[(i + 1) % 2])
                    else:
                        prefetch_next_step()
                    accumulate(b0 + i, ring[i % 2], i == r)

    lam = (jnp.exp(jnp.sum(lq1_ref[...] * lk1_ref[...], axis=-1, keepdims=True))
           - jnp.exp(jnp.sum(lq2_ref[...] * lk2_ref[...], axis=-1, keepdims=True))
           + lam_init)
    ot = acc_ref[0:HEAD_W, :] / acc_ref[HEAD_W:HEAD_W + 1, :]
    dt = ot[:, :TQ] - lam * ot[:, TQ:]
    ms = jnp.mean(dt * dt, axis=0, keepdims=True)
    dn = dt * lax.rsqrt(ms + HEAD_NORM_EPS)
    o_ref[0] = (dn.T * g_ref[...] * (1.0 - lam_init)).astype(BF16)


def _diffattn(dq, dk, dvt, lam_vecs, g, lam_init):
    B, S, W = dq.shape
    TQ = ATT_TQ
    nq = S // TQ
    nk = dvt.shape[2]
    vec = pl.BlockSpec((1, DIFF_DQK), lambda b, h, i: (0, 0))
    return pl.pallas_call(
        functools.partial(_diffattn_kernel, lam_init=lam_init),
        grid=(B, HEADS, S // TQ),
        in_specs=[vec, vec, vec, vec,
                  pl.BlockSpec((1, TQ, HEAD_W), lambda b, h, i: (b, i, h)),
                  pl.BlockSpec((1, TQ, HEAD_W), lambda b, h, i: (b, jnp.minimum(i + 1, nq - 1), h)),
                  pl.BlockSpec((1, S, HEAD_W), lambda b, h, i: (b, 0, h)),
                  pl.BlockSpec((1, 1, nk, HEAD_W, ATT_TK), lambda b, h, i: (b, h, 0, 0, 0)),
                  pl.BlockSpec((1, HEAD_W), lambda b, h, i: (0, h))],
        out_specs=pl.BlockSpec((1, TQ, HEAD_W), lambda b, h, i: (b, i, h)),
        out_shape=jax.ShapeDtypeStruct((B, S, W), BF16),
        scratch_shapes=[pltpu.VMEM((HEAD_W, 2 * TQ), BF16),
                        pltpu.VMEM((ATT_TK, 2 * TQ), F32),
                        pltpu.VMEM((ATT_TK, 2 * TQ), F32),
                        pltpu.VMEM((ATT_TK, 2 * TQ), F32),
                        pltpu.VMEM((1, 2 * TQ), F32),
                        pltpu.VMEM((HEAD_W + ONES_ROWS, 2 * TQ), F32)],
        compiler_params=pltpu.CompilerParams(
            dimension_semantics=("arbitrary", "arbitrary", "arbitrary"),
            vmem_limit_bytes=V7X_VMEM_LIMIT),
        name="diffattn",
    )(*lam_vecs, dq, dq, dk, dvt, g)


def _mixffn_kernel(x_ref, ret_ref, dif_ref, wor_ref, wod_ref, g1_ref, b1_ref,
                   wa_ref, wb_ref, cwa_ref, cwb_ref, cba_ref, cbb_ref, wd_ref, g2_ref, b2_ref,
                   o_ref, x1_ref, carry_ref, ua0_ref, ub0_ref, ua1_ref, ub1_ref, g0_ref, g1s_ref,
                   acc_ref, *, tiles_per_seq):
    T = FFN_ROWS
    C = FFN_COLS
    u_refs = ((ua0_ref, ub0_ref), (ua1_ref, ub1_ref))
    g_refs = (g0_ref, g1s_ref)
    first_of_seq = (pl.program_id(0) % tiles_per_seq) == 0

    @pl.when(first_of_seq)
    def _():
        carry_ref[...] = jnp.zeros_like(carry_ref)

    mix = _dot(ret_ref[...], wor_ref[...]) + _dot(dif_ref[...], wod_ref[...])
    x1 = _layer_norm(DEEPNORM_ALPHA * x_ref[...] + mix, g1_ref[...], b1_ref[...])
    x1_ref[...] = x1
    xb = x1.astype(BF16)

    starts = list(range(0, D_FF, C))
    widths = [min(C, D_FF - s0) for s0 in starts]
    n_tiles = len(starts)

    def up(t):
        s, w = t % 2, widths[t]
        cols = slice(starts[t], starts[t] + w)
        for half, w_ref in enumerate((wa_ref, wb_ref)):
            ccols = slice(half * D_FF + starts[t], half * D_FF + starts[t] + w)
            u = _dot(xb, w_ref[:, cols])
            u_refs[s][half][0:CARRY_ROWS, 0:w] = carry_ref[:, ccols]
            u_refs[s][half][CARRY_ROWS:, 0:w] = u
            carry_ref[:, ccols] = u[T - CARRY_ROWS:, :]

    def conv_half(t, half, cw_ref, cb_ref):
        w = widths[t]
        cols = slice(starts[t], starts[t] + w)
        ue = u_refs[t % 2][half][:, 0:w]
        cw = cw_ref[:, cols]
        y = cw[2:3] * ue + cw[1:2] * pltpu.roll(ue, 1, 0) + cw[0:1] * pltpu.roll(ue, 2, 0)
        return y[CARRY_ROWS:, :] + cb_ref[:, cols]

    def gate(t):
        ya = conv_half(t, 0, cwa_ref, cba_ref)
        yb = conv_half(t, 1, cwb_ref, cbb_ref)
        g_refs[t % 2][:, 0:widths[t]] = (ya * jax.nn.sigmoid(ya) * yb).astype(BF16)

    def down(t):
        w = widths[t]
        part = _dot(g_refs[t % 2][:, 0:w], wd_ref[starts[t]:starts[t] + w, :])
        if t == 0:
            acc_ref[...] = part
        else:
            acc_ref[...] += part

    up(0)
    for t in range(n_tiles + 1):
        if t + 1 < n_tiles:
            up(t + 1)
        if t >= 1:
            down(t - 1)
        if t < n_tiles:
            gate(t)

    o_ref[...] = _layer_norm(DEEPNORM_ALPHA * x1_ref[...] + acc_ref[...],
                             g2_ref[...], b2_ref[...])


def _mixffn(x2, ret2, dif2, w_or, w_od, g1, b1, w_a, w_b, cw_a, cw_b, cb_a, cb_b, w_down,
            g2, b2, seq):
    R, D = x2.shape
    T = FFN_ROWS
    row = lambda i: (i, 0)
    resident = lambda shape: pl.BlockSpec(shape, lambda i: (0, 0), pipeline_mode=pl.Buffered(1))
    return pl.pallas_call(
        functools.partial(_mixffn_kernel, tiles_per_seq=seq // T),
        grid=(R // T,),
        in_specs=[pl.BlockSpec((T, D), row),
                  pl.BlockSpec((T, GROUP_W), row), pl.BlockSpec((T, GROUP_W), row),
                  resident((GROUP_W, D)), resident((GROUP_W, D)),
                  resident((1, D)), resident((1, D)),
                  resident((D, D_FF)), resident((D, D_FF)),
                  resident((3, D_FF)), resident((3, D_FF)),
                  resident((1, D_FF)), resident((1, D_FF)),
                  resident((D_FF, D)),
                  resident((1, D)), resident((1, D))],
        out_specs=pl.BlockSpec((T, D), row),
        out_shape=jax.ShapeDtypeStruct((R, D), F32),
        scratch_shapes=[pltpu.VMEM((T, D), F32),
                        pltpu.VMEM((CARRY_ROWS, 2 * D_FF), F32),
                        *[pltpu.VMEM((CARRY_ROWS + T, FFN_COLS), F32)] * 4,
                        *[pltpu.VMEM((T, FFN_COLS), BF16)] * 2,
                        pltpu.VMEM((T, D), F32)],
        compiler_params=pltpu.CompilerParams(
            dimension_semantics=("arbitrary",), vmem_limit_bytes=V7X_VMEM_LIMIT),
        name="mixffn",
    )(x2, ret2, dif2, w_or, w_od, g1, b1, w_a, w_b, cw_a, cw_b, cb_a, cb_b, w_down, g2, b2)


def _rotary_tables(seq):
    half = HEAD_W // 2
    inv_freq = ROPE_BASE ** (-jnp.arange(half, dtype=F32) / half)
    ang = jnp.arange(seq).astype(F32)[:, None] * inv_freq[None, :]
    cos, sin = jnp.cos(ang), jnp.sin(ang)
    return (jnp.concatenate([cos, cos], axis=1), jnp.concatenate([-sin, sin], axis=1),
            cos.T, sin.T)


def kernel(x, w_in, ret_gn_g, diff_norm_g, lam_q1, lam_k1, lam_q2, lam_k2, w_out,
           ln1_g, ln1_b, w_up, conv_w, conv_b, w_down, ln2_g, ln2_b):
    B, S, D = x.shape
    tabs = _rotary_tables(S)
    G = GROUP_W
    for l in range(DEPTH):
        lam_init = 0.8 - 0.6 * math.exp(-0.3 * l)
        wl = w_in[l].astype(BF16)
        w = (wl[:, 0:G], wl[:, G:2 * G].T, wl[:, 2 * G:3 * G], wl[:, 3 * G:4 * G],
             wl[:, 4 * G:5 * G], wl[:, 5 * G:6 * G], wl[:, 6 * G:7 * G].T)
        rq, rkt, rv, rg, dq, dk, dvt = _proj(x, tabs, w)
        ret = _retention(rq, rkt, rv, rg, ret_gn_g[l][None, :])
        lam_vecs = (lam_q1[l][None, :], lam_k1[l][None, :], lam_q2[l][None, :], lam_k2[l][None, :])
        dif = _diffattn(dq, dk, dvt, lam_vecs, diff_norm_g[l][None, :], lam_init)
        wo = w_out[l].astype(BF16)
        wu = w_up[l].astype(BF16)
        x2 = _mixffn(x.reshape(B * S, D), ret.reshape(B * S, G), dif.reshape(B * S, G),
                     wo[:G], wo[G:], ln1_g[l][None, :], ln1_b[l][None, :],
                     wu[:, :D_FF], wu[:, D_FF:], conv_w[l][:, :D_FF], conv_w[l][:, D_FF:],
                     conv_b[l][None, :D_FF], conv_b[l][None, D_FF:], w_down[l].astype(BF16),
                     ln2_g[l][None, :], ln2_b[l][None, :], S)
        x = x2.reshape(B, S, D)
    return x
```

```python
import functools
import math

import jax
import jax.numpy as jnp
from jax import lax
from jax.experimental import pallas as pl
from jax.experimental.pallas import tpu as pltpu

F32 = jnp.float32
BF16 = jnp.bfloat16

D_MODEL = 1024
DEPTH = 2
HEADS = 4
HEAD_W = 128
GROUP_W = HEADS * HEAD_W
RET_CHUNK = 128
DIFF_DQK = 64
D_FF = 2816
ROPE_BASE = 10000.0
LN_EPS = 1e-5
HEAD_NORM_EPS = 1e-5
DEEPNORM_ALPHA = (2.0 * DEPTH) ** 0.25

V7X_VMEM_LIMIT = 56 * 1024 * 1024

PROJ_ROWS = 512
RET_ROWS = 512
ATT_TK = PROJ_ROWS
ATT_TQ = ATT_TK
FFN_ROWS = 512
FFN_COLS = 512
CARRY_ROWS = 8
ONES_ROWS = 16
DIFF_Q_SCALE = DIFF_DQK ** -0.5 * math.log2(math.e)
SHIFT_MARGIN = 1.02
DENOM_FLOOR = 2.0 ** -60

COL_RQ, COL_RK, COL_RV, COL_RG, COL_DQ, COL_DK, COL_DV = range(7)

NT_DIMS = (((1,), (1,)), ((), ()))


def _dot(a, b):
    return jnp.dot(a, b, preferred_element_type=F32)


def _dot_nt(a, b):
    return lax.dot_general(a, b, NT_DIMS, preferred_element_type=F32)


def _layer_norm(y, g, b):
    mu = jnp.mean(y, axis=-1, keepdims=True)
    yc = y - mu
    var = jnp.mean(yc * yc, axis=-1, keepdims=True)
    return yc * lax.rsqrt(var + LN_EPS) * g + b


def _proj_kernel(x_ref, cs_ref, sn_ref, cst_ref, snt_ref, w_ref, wrkt_ref, wdvt_ref,
                 rq_ref, rkt_ref, rv_ref, rg_ref, dq_ref, dk_ref, dvt_ref):
    T = PROJ_ROWS
    half = HEAD_W // 2
    xb = x_ref[0].astype(BF16)

    def group(c):
        return _dot(xb, w_ref[:, c * GROUP_W:(c + 1) * GROUP_W])

    cs = cs_ref[...]
    sn = sn_ref[...]
    t = group(COL_RQ)
    for h in range(HEADS):
        th = t[:, h * HEAD_W:(h + 1) * HEAD_W]
        rq_ref[0, :, h * HEAD_W:(h + 1) * HEAD_W] = (
            th * cs + pltpu.roll(th, half, 1) * sn).astype(BF16)
    cst = cst_ref[...]
    snt = snt_ref[...]
    tt = _dot_nt(wrkt_ref[...], xb)
    kscale = HEAD_W ** -0.5
    for h in range(HEADS):
        a = tt[h * HEAD_W:h * HEAD_W + half]
        b = tt[h * HEAD_W + half:(h + 1) * HEAD_W]
        rkt_ref[0, h * HEAD_W:h * HEAD_W + half, :] = ((a * cst - b * snt) * kscale).astype(BF16)
        rkt_ref[0, h * HEAD_W + half:(h + 1) * HEAD_W, :] = ((a * snt + b * cst) * kscale).astype(BF16)
    rv_ref[0] = group(COL_RV).astype(BF16)
    rg_ref[0] = group(COL_RG)
    dq_ref[0] = (group(COL_DQ) * DIFF_Q_SCALE).astype(BF16)
    dk_ref[0] = group(COL_DK).astype(BF16)
    dvt = _dot_nt(wdvt_ref[...], xb).astype(BF16)
    dvt_ref[0, :, 0] = dvt.reshape(HEADS, HEAD_W, T)


def _proj(x, tabs, w, wrkt, wdvt):
    B, S, D = x.shape
    T = PROJ_ROWS
    n = S // T
    row = lambda b, i: (b, i, 0)
    const2 = lambda b, i: (0, 0)
    wspec = lambda a: pl.BlockSpec(a.shape, const2)
    out_shapes = (
        jax.ShapeDtypeStruct((B, S, GROUP_W), BF16),
        jax.ShapeDtypeStruct((B, GROUP_W, S), BF16),
        jax.ShapeDtypeStruct((B, S, GROUP_W), BF16),
        jax.ShapeDtypeStruct((B, S, GROUP_W), F32),
        jax.ShapeDtypeStruct((B, S, GROUP_W), BF16),
        jax.ShapeDtypeStruct((B, S, GROUP_W), BF16),
        jax.ShapeDtypeStruct((B, HEADS, n, HEAD_W, T), BF16),
    )
    rows = pl.BlockSpec((1, T, GROUP_W), row)
    return pl.pallas_call(
        _proj_kernel,
        grid=(B, n),
        in_specs=[pl.BlockSpec((1, T, D), row),
                  pl.BlockSpec((T, HEAD_W), lambda b, i: (i, 0)),
                  pl.BlockSpec((T, HEAD_W), lambda b, i: (i, 0)),
                  pl.BlockSpec((HEAD_W // 2, T), lambda b, i: (0, i)),
                  pl.BlockSpec((HEAD_W // 2, T), lambda b, i: (0, i)),
                  wspec(w), wspec(wrkt), wspec(wdvt)],
        out_specs=(
            rows,
            pl.BlockSpec((1, GROUP_W, T), lambda b, i: (b, 0, i)),
            rows, rows, rows, rows,
            pl.BlockSpec((1, HEADS, 1, HEAD_W, T), lambda b, i: (b, 0, i, 0, 0)),
        ),
        out_shape=out_shapes,
        compiler_params=pltpu.CompilerParams(
            dimension_semantics=("arbitrary", "arbitrary"),
            vmem_limit_bytes=V7X_VMEM_LIMIT),
        name="proj",
    )(x, *tabs, w, wrkt, wdvt)


def _retention_kernel(q_ref, kt_ref, v_ref, g_ref, gn_ref, o_ref, state_ref):
    C = RET_CHUNK

    @pl.when(pl.program_id(1) == 0)
    def _():
        state_ref[...] = jnp.zeros_like(state_ref)

    ri = lax.broadcasted_iota(jnp.int32, (C, C), 0)
    ci = lax.broadcasted_iota(jnp.int32, (C, C), 1)
    rel = (ri - ci).astype(F32)
    gn = gn_ref[...]
    for h in range(HEADS):
        log_g = math.log(1.0 - 2.0 ** (-5.0 - h))
        d_intra = jnp.where(ri >= ci, jnp.exp(log_g * jnp.maximum(rel, 0.0)), 0.0)
        q_dec = jnp.exp(log_g * (ri.astype(F32) + 1.0))
        k_dec = jnp.exp(log_g * (C - 1.0 - ci.astype(F32)))
        c_dec = math.exp(log_g * C)
        cols = slice(h * HEAD_W, (h + 1) * HEAD_W)
        state = state_ref[h]
        for c in range(RET_ROWS // C):
            rws = slice(c * C, (c + 1) * C)
            qc = q_ref[0, rws, cols]
            ktc = kt_ref[0, cols, rws]
            vc = v_ref[0, rws, cols]
            s = _dot(qc, ktc) * d_intra
            inner = _dot(s.astype(BF16), vc)
            cross = _dot((qc.astype(F32) * q_dec).astype(BF16), state.astype(BF16))
            state = state * c_dec + _dot((ktc.astype(F32) * k_dec).astype(BF16), vc)
            y = inner + cross
            mu = jnp.mean(y, axis=-1, keepdims=True)
            yc = y - mu
            var = jnp.mean(yc * yc, axis=-1, keepdims=True)
            yn = yc * lax.rsqrt(var + HEAD_NORM_EPS) * gn[:, cols]
            gate = g_ref[0, rws, cols]
            o_ref[0, rws, cols] = (yn * (gate * jax.nn.sigmoid(gate))).astype(BF16)
        state_ref[h] = state


def _retention(rq, rkt, rv, rg, gn):
    B, S, W = rq.shape
    T = RET_ROWS
    row = lambda b, i: (b, i, 0)
    rows = pl.BlockSpec((1, T, W), row)
    return pl.pallas_call(
        _retention_kernel,
        grid=(B, S // T),
        in_specs=[rows, pl.BlockSpec((1, W, T), lambda b, i: (b, 0, i)), rows, rows,
                  pl.BlockSpec((1, W), lambda b, i: (0, 0))],
        out_specs=rows,
        out_shape=jax.ShapeDtypeStruct((B, S, W), BF16),
        scratch_shapes=[pltpu.VMEM((HEADS, HEAD_W, HEAD_W), F32)],
        compiler_params=pltpu.CompilerParams(
            dimension_semantics=("arbitrary", "arbitrary"),
            vmem_limit_bytes=V7X_VMEM_LIMIT),
        name="retention",
    )(rq, rkt, rv, rg, gn)


def _diffattn_kernel(lq1_ref, lk1_ref, lq2_ref, lk2_ref, q_ref, qn_ref, k_ref, vt_ref, g_ref,
                     o_ref, qs_ref, kabs_ref, pa_ref, pb_ref, pc_ref, acc_ref,
                     qs2_ref, s2_ref, m2_ref, acc2_ref, *, lam_init):
    TQ, TK = ATT_TQ, ATT_TK
    qi = pl.program_id(2)
    lane = lax.broadcasted_iota(jnp.int32, (TK, HEAD_W), 1)
    one_at = lambda c: jnp.where(lane == c, 1.0, 0.0).astype(BF16)
    ones_rows = jnp.where(lax.broadcasted_iota(jnp.int32, (ONES_ROWS, TK), 0) == 0,
                          1.0, 0.0).astype(BF16)

    def split_queries(src_ref):
        qt = src_ref[0].astype(F32).T
        chan = lax.broadcasted_iota(jnp.int32, qt.shape, 0)
        return (jnp.where(chan < DIFF_DQK, qt, 0.0).astype(BF16),
                jnp.where(chan >= DIFF_DQK, qt, 0.0).astype(BF16), chan)

    @pl.when(qi == 0)
    def _():
        kabs = jnp.max(jnp.abs(k_ref[0].astype(F32)), axis=0, keepdims=True)
        kabs_ref[...] = jnp.broadcast_to(kabs, kabs_ref.shape).astype(BF16)

    def load_queries(src_ref):
        q1, q2, chan = split_queries(src_ref)
        for half, (qh, shift_row) in enumerate(((q1, DIFF_DQK), (q2, 0))):
            bound = _dot(kabs_ref[...], jnp.abs(qh))[0:1]
            shift = (bound * SHIFT_MARGIN).astype(BF16).astype(F32)
            w = jnp.where(chan == shift_row, -shift, qh.astype(F32))
            qs_ref[:, half * TQ:(half + 1) * TQ] = w.astype(BF16)

    def probs(j, p_ref, masked):
        start = pl.multiple_of(j * TK, TK)
        kj = k_ref[0, pl.ds(start, TK), :]
        k1 = jnp.where(lane < DIFF_DQK, kj, one_at(DIFF_DQK))
        k2 = jnp.where(lane >= DIFF_DQK, kj, one_at(0))
        for half, kh in enumerate((k1, k2)):
            st = _dot(kh, qs_ref[:, half * TQ:(half + 1) * TQ])
            if masked:
                kpos = j * TK + lax.broadcasted_iota(jnp.int32, st.shape, 0)
                qpos = qi * TQ + lax.broadcasted_iota(jnp.int32, st.shape, 1)
                st = jnp.where(kpos <= qpos, st, -jnp.inf)
            p_ref[:, half * TQ:(half + 1) * TQ] = jnp.exp2(st).astype(BF16)

    def values(j, p_ref):
        vt = jnp.concatenate([vt_ref[0, 0, j], ones_rows], axis=0)
        acc_ref[...] += _dot(vt, p_ref[...])

    acc_ref[...] = jnp.zeros_like(acc_ref)

    def prefetch_next_step():
        load_queries(qn_ref)
        probs(0, pc_ref, False)

    @pl.when(qi == 0)
    def _():
        load_queries(q_ref)
        probs(0, pc_ref, True)
        values(0, pc_ref)
        prefetch_next_step()

    @pl.when(qi == 1)
    def _():
        probs(1, pb_ref, True)
        values(0, pc_ref)
        prefetch_next_step()
        values(1, pb_ref)

    @pl.when(qi >= 2)
    def _():
        ring = (pb_ref, pa_ref)
        probs(1, pb_ref, False)
        values(0, pc_ref)
        rest = qi - 2

        def run(b, n):
            for i in range(n):
                probs(b + i + 1, ring[(i + 1) % 2], False)
                values(b + i, ring[i % 2])

        def octet(t, carry):
            run(1 + 8 * t, 8)
            return carry

        lax.fori_loop(0, rest // 8, octet, 0)

        @pl.when(rest % 8 >= 4)
        def _():
            run(1 + 8 * (rest // 8), 4)

        b0 = 1 + 4 * (rest // 4)
        for r in range(4):
            @pl.when(rest % 4 == r)
            def _(r=r):
                run(b0, r)
                probs(b0 + r + 1, ring[(r + 1) % 2], True)
                prefetch_next_step()
                values(b0 + r, ring[r % 2])
                values(b0 + r + 1, ring[(r + 1) % 2])

    def finish(a_ref):
        lam = (jnp.exp(jnp.sum(lq1_ref[...] * lk1_ref[...], axis=-1, keepdims=True))
               - jnp.exp(jnp.sum(lq2_ref[...] * lk2_ref[...], axis=-1, keepdims=True))
               + lam_init)
        ot = a_ref[0:HEAD_W, :] / a_ref[HEAD_W:HEAD_W + 1, :]
        dt = ot[:, :TQ] - lam * ot[:, TQ:]
        ms = jnp.mean(dt * dt, axis=0, keepdims=True)
        dn = dt * lax.rsqrt(ms + HEAD_NORM_EPS)
        o_ref[0] = (dn.T * g_ref[...] * (1.0 - lam_init)).astype(BF16)

    finish(acc_ref)

    denom_min = jnp.min(acc_ref[HEAD_W:HEAD_W + 1, :])

    @pl.when(jnp.logical_not(denom_min >= DENOM_FLOOR))
    def _():
        q1, q2, _ = split_queries(q_ref)
        qs2_ref[...] = jnp.concatenate([q1, q2], axis=1)
        m2_ref[...] = jnp.full_like(m2_ref, -jnp.inf)
        acc2_ref[...] = jnp.zeros_like(acc2_ref)

        def block(j, carry):
            start = pl.multiple_of(j * TK, TK)
            s2_ref[...] = _dot(k_ref[0, pl.ds(start, TK), :], qs2_ref[...])
            st = s2_ref[...]
            kpos = j * TK + lax.broadcasted_iota(jnp.int32, st.shape, 0)
            col = lax.broadcasted_iota(jnp.int32, st.shape, 1)
            qpos = qi * TQ + jnp.where(col >= TQ, col - TQ, col)
            st = jnp.where(kpos <= qpos, st, -jnp.inf)
            m_prev = m2_ref[...]
            m_new = jnp.maximum(m_prev, jnp.max(st, axis=0, keepdims=True))
            p = jnp.exp2(st - m_new).astype(BF16)
            vt = jnp.concatenate([vt_ref[0, 0, j], ones_rows], axis=0)
            acc2_ref[...] = acc2_ref[...] * jnp.exp2(m_prev - m_new) + _dot(vt, p)
            m2_ref[...] = m_new
            return carry

        lax.fori_loop(0, qi + 1, block, 0)
        finish(acc2_ref)


def _diffattn(dq, dk, dvt, lam_vecs, g, lam_init):
    B, S, W = dq.shape
    TQ = ATT_TQ
    nq = S // TQ
    nk = dvt.shape[2]
    vec = pl.BlockSpec((1, DIFF_DQK), lambda b, h, i: (0, 0))
    return pl.pallas_call(
        functools.partial(_diffattn_kernel, lam_init=lam_init),
        grid=(B, HEADS, S // TQ),
        in_specs=[vec, vec, vec, vec,
                  pl.BlockSpec((1, TQ, HEAD_W), lambda b, h, i: (b, i, h)),
                  pl.BlockSpec((1, TQ, HEAD_W), lambda b, h, i: (b, jnp.minimum(i + 1, nq - 1), h)),
                  pl.BlockSpec((1, S, HEAD_W), lambda b, h, i: (b, 0, h)),
                  pl.BlockSpec((1, 1, nk, HEAD_W, ATT_TK), lambda b, h, i: (b, h, 0, 0, 0)),
                  pl.BlockSpec((1, HEAD_W), lambda b, h, i: (0, h))],
        out_specs=pl.BlockSpec((1, TQ, HEAD_W), lambda b, h, i: (b, i, h)),
        out_shape=jax.ShapeDtypeStruct((B, S, W), BF16),
        scratch_shapes=[pltpu.VMEM((HEAD_W, 2 * TQ), BF16),
                        pltpu.VMEM((ONES_ROWS, HEAD_W), BF16),
                        pltpu.VMEM((ATT_TK, 2 * TQ), BF16),
                        pltpu.VMEM((ATT_TK, 2 * TQ), BF16),
                        pltpu.VMEM((ATT_TK, 2 * TQ), BF16),
                        pltpu.VMEM((HEAD_W + ONES_ROWS, 2 * TQ), F32),
                        pltpu.VMEM((HEAD_W, 2 * TQ), BF16),
                        pltpu.VMEM((ATT_TK, 2 * TQ), F32),
                        pltpu.VMEM((1, 2 * TQ), F32),
                        pltpu.VMEM((HEAD_W + ONES_ROWS, 2 * TQ), F32)],
        compiler_params=pltpu.CompilerParams(
            dimension_semantics=("arbitrary", "arbitrary", "arbitrary"),
            vmem_limit_bytes=V7X_VMEM_LIMIT),
        name="diffattn",
    )(*lam_vecs, dq, dq, dk, dvt, g)


def _mixffn_kernel(x_ref, ret_ref, dif_ref, wo_ref, g1_ref, b1_ref,
                   wu_ref, cw_ref, cb_ref, wd_ref, g2_ref, b2_ref,
                   o_ref, x1_ref, carry_ref, ua0_ref, ub0_ref, ua1_ref, ub1_ref, g0_ref, g1s_ref,
                   acc_ref, *, tiles_per_seq):
    T = FFN_ROWS
    C = FFN_COLS
    u_refs = ((ua0_ref, ub0_ref), (ua1_ref, ub1_ref))
    g_refs = (g0_ref, g1s_ref)
    first_of_seq = (pl.program_id(0) % tiles_per_seq) == 0

    @pl.when(first_of_seq)
    def _():
        carry_ref[...] = jnp.zeros_like(carry_ref)

    mix = (_dot(ret_ref[...], wo_ref[0:GROUP_W, :])
           + _dot(dif_ref[...], wo_ref[GROUP_W:2 * GROUP_W, :]))
    x1 = _layer_norm(DEEPNORM_ALPHA * x_ref[...] + mix, g1_ref[...], b1_ref[...])
    x1_ref[...] = x1
    xb = x1.astype(BF16)

    starts = list(range(0, D_FF, C))
    widths = [min(C, D_FF - s0) for s0 in starts]
    n_tiles = len(starts)

    def cols(t, half):
        return slice(half * D_FF + starts[t], half * D_FF + starts[t] + widths[t])

    def up(t):
        s, w = t % 2, widths[t]
        for half in range(2):
            u = _dot(xb, wu_ref[:, cols(t, half)])
            u_refs[s][half][0:CARRY_ROWS, 0:w] = carry_ref[:, cols(t, half)]
            u_refs[s][half][CARRY_ROWS:, 0:w] = u
            carry_ref[:, cols(t, half)] = u[T - CARRY_ROWS:, :]

    def conv_half(t, half):
        ue = u_refs[t % 2][half][:, 0:widths[t]]
        cw = cw_ref[:, cols(t, half)]
        y = cw[2:3] * ue + cw[1:2] * pltpu.roll(ue, 1, 0) + cw[0:1] * pltpu.roll(ue, 2, 0)
        return y[CARRY_ROWS:, :] + cb_ref[:, cols(t, half)]

    def gate(t):
        ya = conv_half(t, 0)
        yb = conv_half(t, 1)
        g_refs[t % 2][:, 0:widths[t]] = (ya * jax.nn.sigmoid(ya) * yb).astype(BF16)

    def down(t):
        w = widths[t]
        part = _dot(g_refs[t % 2][:, 0:w], wd_ref[starts[t]:starts[t] + w, :])
        if t == 0:
            acc_ref[...] = part
        else:
            acc_ref[...] += part

    up(0)
    for t in range(n_tiles + 1):
        if t + 1 < n_tiles:
            up(t + 1)
        if t >= 1:
            down(t - 1)
        if t < n_tiles:
            gate(t)

    o_ref[...] = _layer_norm(DEEPNORM_ALPHA * x1_ref[...] + acc_ref[...],
                             g2_ref[...], b2_ref[...])


def _mixffn(x2, ret2, dif2, w_out, g1, b1, w_up, conv_w, conv_b, w_down, g2, b2, seq):
    R, D = x2.shape
    T = FFN_ROWS
    row = lambda i: (i, 0)
    resident = lambda a: pl.BlockSpec(a.shape, lambda i: (0, 0), pipeline_mode=pl.Buffered(1))
    return pl.pallas_call(
        functools.partial(_mixffn_kernel, tiles_per_seq=seq // T),
        grid=(R // T,),
        in_specs=[pl.BlockSpec((T, D), row),
                  pl.BlockSpec((T, GROUP_W), row), pl.BlockSpec((T, GROUP_W), row),
                  resident(w_out), resident(g1), resident(b1),
                  resident(w_up), resident(conv_w), resident(conv_b), resident(w_down),
                  resident(g2), resident(b2)],
        out_specs=pl.BlockSpec((T, D), row),
        out_shape=jax.ShapeDtypeStruct((R, D), F32),
        scratch_shapes=[pltpu.VMEM((T, D), F32),
                        pltpu.VMEM((CARRY_ROWS, 2 * D_FF), F32),
                        *[pltpu.VMEM((CARRY_ROWS + T, FFN_COLS), F32)] * 4,
                        *[pltpu.VMEM((T, FFN_COLS), BF16)] * 2,
                        pltpu.VMEM((T, D), F32)],
        compiler_params=pltpu.CompilerParams(
            dimension_semantics=("arbitrary",), vmem_limit_bytes=V7X_VMEM_LIMIT),
        name="mixffn",
    )(x2, ret2, dif2, w_out, g1, b1, w_up, conv_w, conv_b, w_down, g2, b2)


def _rotary_tables(seq):
    half = HEAD_W // 2
    inv_freq = ROPE_BASE ** (-jnp.arange(half, dtype=F32) / half)
    ang = jnp.arange(seq).astype(F32)[:, None] * inv_freq[None, :]
    cos, sin = jnp.cos(ang), jnp.sin(ang)
    return (jnp.concatenate([cos, cos], axis=1), jnp.concatenate([-sin, sin], axis=1),
            cos.T, sin.T)


def kernel(x, w_in, ret_gn_g, diff_norm_g, lam_q1, lam_k1, lam_q2, lam_k2, w_out,
           ln1_g, ln1_b, w_up, conv_w, conv_b, w_down, ln2_g, ln2_b):
    B, S, D = x.shape
    tabs = _rotary_tables(S)
    G = GROUP_W
    for l in range(DEPTH):
        lam_init = 0.8 - 0.6 * math.exp(-0.3 * l)
        wl = w_in[l].astype(BF16)
        rq, rkt, rv, rg, dq, dk, dvt = _proj(
            x, tabs, wl, wl[:, COL_RK * G:(COL_RK + 1) * G].T, wl[:, COL_DV * G:(COL_DV + 1) * G].T)
        ret = _retention(rq, rkt, rv, rg, ret_gn_g[l][None, :])
        lam_vecs = (lam_q1[l][None, :], lam_k1[l][None, :], lam_q2[l][None, :], lam_k2[l][None, :])
        dif = _diffattn(dq, dk, dvt, lam_vecs, diff_norm_g[l][None, :], lam_init)
        x2 = _mixffn(x.reshape(B * S, D), ret.reshape(B * S, G), dif.reshape(B * S, G),
                     w_out[l].astype(BF16), ln1_g[l][None, :], ln1_b[l][None, :],
                     w_up[l].astype(BF16), conv_w[l], conv_b[l][None, :], w_down[l].astype(BF16),
                     ln2_g[l][None, :], ln2_b[l][None, :], S)
        x = x2.reshape(B, S, D)
    return x
```

```python
import functools
import math

import jax
import jax.numpy as jnp
from jax import lax
from jax.experimental import pallas as pl
from jax.experimental.pallas import tpu as pltpu

F32 = jnp.float32
BF16 = jnp.bfloat16

D_MODEL = 1024
DEPTH = 2
HEADS = 4
HEAD_W = 128
GROUP_W = HEADS * HEAD_W
RET_CHUNK = 128
DIFF_DQK = 64
D_FF = 2816
ROPE_BASE = 10000.0
LN_EPS = 1e-5
HEAD_NORM_EPS = 1e-5
DEEPNORM_ALPHA = (2.0 * DEPTH) ** 0.25

V7X_VMEM_LIMIT = 56 * 1024 * 1024

PROJ_ROWS = 512
RET_ROWS = 512
ATT_TK = PROJ_ROWS
ATT_TQ = ATT_TK
FFN_ROWS = 512
FFN_COLS = 1024
CARRY_ROWS = 8
ONES_ROWS = 16
DIFF_Q_SCALE = DIFF_DQK ** -0.5 * math.log2(math.e)
SHIFT_MARGIN = 1.02
DENOM_FLOOR = 2.0 ** -60

COL_RQ, COL_RK, COL_RV, COL_RG, COL_DQ, COL_DK, COL_DV = range(7)

NT_DIMS = (((1,), (1,)), ((), ()))


def _dot(a, b):
    return jnp.dot(a, b, preferred_element_type=F32)


def _dot_nt(a, b):
    return lax.dot_general(a, b, NT_DIMS, preferred_element_type=F32)


def _layer_norm(y, g, b):
    mu = jnp.mean(y, axis=-1, keepdims=True)
    yc = y - mu
    var = jnp.mean(yc * yc, axis=-1, keepdims=True)
    return yc * lax.rsqrt(var + LN_EPS) * g + b


def _proj_kernel(x_ref, cs_ref, sn_ref, cst_ref, snt_ref, w_ref, wrkt_ref, wdvt_ref,
                 rq_ref, rkt_ref, rv_ref, rg_ref, dq_ref, dk_ref, dvt_ref):
    T = PROJ_ROWS
    half = HEAD_W // 2
    xb = x_ref[0].astype(BF16)

    def group(c):
        return _dot(xb, w_ref[:, c * GROUP_W:(c + 1) * GROUP_W])

    cs = cs_ref[...]
    sn = sn_ref[...]
    t = group(COL_RQ)
    for h in range(HEADS):
        th = t[:, h * HEAD_W:(h + 1) * HEAD_W]
        rq_ref[0, :, h * HEAD_W:(h + 1) * HEAD_W] = (
            th * cs + pltpu.roll(th, half, 1) * sn).astype(BF16)
    cst = cst_ref[...]
    snt = snt_ref[...]
    tt = _dot_nt(wrkt_ref[...], xb)
    kscale = HEAD_W ** -0.5
    for h in range(HEADS):
        a = tt[h * HEAD_W:h * HEAD_W + half]
        b = tt[h * HEAD_W + half:(h + 1) * HEAD_W]
        rkt_ref[0, h * HEAD_W:h * HEAD_W + half, :] = ((a * cst - b * snt) * kscale).astype(BF16)
        rkt_ref[0, h * HEAD_W + half:(h + 1) * HEAD_W, :] = ((a * snt + b * cst) * kscale).astype(BF16)
    rv_ref[0] = group(COL_RV).astype(BF16)
    rg_ref[0] = group(COL_RG)
    dq_ref[0] = (group(COL_DQ) * DIFF_Q_SCALE).astype(BF16)
    dk_ref[0] = group(COL_DK).astype(BF16)
    dvt = _dot_nt(wdvt_ref[...], xb).astype(BF16)
    dvt_ref[0, :, 0] = dvt.reshape(HEADS, HEAD_W, T)


def _proj(x, tabs, w, wrkt, wdvt):
    B, S, D = x.shape
    T = PROJ_ROWS
    n = S // T
    row = lambda b, i: (b, i, 0)
    const2 = lambda b, i: (0, 0)
    wspec = lambda a: pl.BlockSpec(a.shape, const2)
    out_shapes = (
        jax.ShapeDtypeStruct((B, S, GROUP_W), BF16),
        jax.ShapeDtypeStruct((B, GROUP_W, S), BF16),
        jax.ShapeDtypeStruct((B, S, GROUP_W), BF16),
        jax.ShapeDtypeStruct((B, S, GROUP_W), F32),
        jax.ShapeDtypeStruct((B, S, GROUP_W), BF16),
        jax.ShapeDtypeStruct((B, S, GROUP_W), BF16),
        jax.ShapeDtypeStruct((B, HEADS, n, HEAD_W, T), BF16),
    )
    rows = pl.BlockSpec((1, T, GROUP_W), row)
    return pl.pallas_call(
        _proj_kernel,
        grid=(B, n),
        in_specs=[pl.BlockSpec((1, T, D), row),
                  pl.BlockSpec((T, HEAD_W), lambda b, i: (i, 0)),
                  pl.BlockSpec((T, HEAD_W), lambda b, i: (i, 0)),
                  pl.BlockSpec((HEAD_W // 2, T), lambda b, i: (0, i)),
                  pl.BlockSpec((HEAD_W // 2, T), lambda b, i: (0, i)),
                  wspec(w), wspec(wrkt), wspec(wdvt)],
        out_specs=(
            rows,
            pl.BlockSpec((1, GROUP_W, T), lambda b, i: (b, 0, i)),
            rows, rows, rows, rows,
            pl.BlockSpec((1, HEADS, 1, HEAD_W, T), lambda b, i: (b, 0, i, 0, 0)),
        ),
        out_shape=out_shapes,
        compiler_params=pltpu.CompilerParams(
            dimension_semantics=("arbitrary", "arbitrary"),
            vmem_limit_bytes=V7X_VMEM_LIMIT),
        name="proj",
    )(x, *tabs, w, wrkt, wdvt)


def _retention_kernel(q_ref, kt_ref, v_ref, g_ref, gn_ref, o_ref, state_ref):
    C = RET_CHUNK

    @pl.when(pl.program_id(1) == 0)
    def _():
        state_ref[...] = jnp.zeros_like(state_ref)

    ri = lax.broadcasted_iota(jnp.int32, (C, C), 0)
    ci = lax.broadcasted_iota(jnp.int32, (C, C), 1)
    rel = (ri - ci).astype(F32)
    gn = gn_ref[...]
    for h in range(HEADS):
        log_g = math.log(1.0 - 2.0 ** (-5.0 - h))
        d_intra = jnp.where(ri >= ci, jnp.exp(log_g * jnp.maximum(rel, 0.0)), 0.0)
        q_dec = jnp.exp(log_g * (ri.astype(F32) + 1.0))
        k_dec = jnp.exp(log_g * (C - 1.0 - ci.astype(F32)))
        c_dec = math.exp(log_g * C)
        cols = slice(h * HEAD_W, (h + 1) * HEAD_W)
        state = state_ref[h]
        for c in range(RET_ROWS // C):
            rws = slice(c * C, (c + 1) * C)
            qc = q_ref[0, rws, cols]
            ktc = kt_ref[0, cols, rws]
            vc = v_ref[0, rws, cols]
            s = _dot(qc, ktc) * d_intra
            inner = _dot(s.astype(BF16), vc)
            cross = _dot((qc.astype(F32) * q_dec).astype(BF16), state.astype(BF16))
            state = state * c_dec + _dot((ktc.astype(F32) * k_dec).astype(BF16), vc)
            y = inner + cross
            mu = jnp.mean(y, axis=-1, keepdims=True)
            yc = y - mu
            var = jnp.mean(yc * yc, axis=-1, keepdims=True)
            yn = yc * lax.rsqrt(var + HEAD_NORM_EPS) * gn[:, cols]
            gate = g_ref[0, rws, cols]
            o_ref[0, rws, cols] = (yn * (gate * jax.nn.sigmoid(gate))).astype(BF16)
        state_ref[h] = state


def _retention(rq, rkt, rv, rg, gn):
    B, S, W = rq.shape
    T = RET_ROWS
    row = lambda b, i: (b, i, 0)
    rows = pl.BlockSpec((1, T, W), row)
    return pl.pallas_call(
        _retention_kernel,
        grid=(B, S // T),
        in_specs=[rows, pl.BlockSpec((1, W, T), lambda b, i: (b, 0, i)), rows, rows,
                  pl.BlockSpec((1, W), lambda b, i: (0, 0))],
        out_specs=rows,
        out_shape=jax.ShapeDtypeStruct((B, S, W), BF16),
        scratch_shapes=[pltpu.VMEM((HEADS, HEAD_W, HEAD_W), F32)],
        compiler_params=pltpu.CompilerParams(
            dimension_semantics=("arbitrary", "arbitrary"),
            vmem_limit_bytes=V7X_VMEM_LIMIT),
        name="retention",
    )(rq, rkt, rv, rg, gn)


def _diffattn_kernel(lq1_ref, lk1_ref, lq2_ref, lk2_ref, q_ref, qn_ref, k_ref, vt_ref, g_ref,
                     o_ref, qs_ref, kabs_ref, pa_ref, pb_ref, pc_ref, acc_ref,
                     qs2_ref, s2_ref, m2_ref, acc2_ref, *, lam_init):
    TQ, TK = ATT_TQ, ATT_TK
    qi = pl.program_id(2)
    lane = lax.broadcasted_iota(jnp.int32, (TK, HEAD_W), 1)
    one_at = lambda c: jnp.where(lane == c, 1.0, 0.0).astype(BF16)
    ones_rows = jnp.where(lax.broadcasted_iota(jnp.int32, (ONES_ROWS, TK), 0) == 0,
                          1.0, 0.0).astype(BF16)

    def split_queries(src_ref):
        qt = src_ref[0].astype(F32).T
        chan = lax.broadcasted_iota(jnp.int32, qt.shape, 0)
        return (jnp.where(chan < DIFF_DQK, qt, 0.0).astype(BF16),
                jnp.where(chan >= DIFF_DQK, qt, 0.0).astype(BF16), chan)

    @pl.when(qi == 0)
    def _():
        kabs = jnp.max(jnp.abs(k_ref[0].astype(F32)), axis=0, keepdims=True)
        kabs_ref[...] = jnp.broadcast_to(kabs, kabs_ref.shape).astype(BF16)

    def load_queries(src_ref):
        q1, q2, chan = split_queries(src_ref)
        for half, (qh, shift_row) in enumerate(((q1, DIFF_DQK), (q2, 0))):
            bound = _dot(kabs_ref[...], jnp.abs(qh))[0:1]
            shift = (bound * SHIFT_MARGIN).astype(BF16).astype(F32)
            w = jnp.where(chan == shift_row, -shift, qh.astype(F32))
            qs_ref[:, half * TQ:(half + 1) * TQ] = w.astype(BF16)

    def probs(j, p_ref, masked):
        start = pl.multiple_of(j * TK, TK)
        kj = k_ref[0, pl.ds(start, TK), :]
        k1 = jnp.where(lane < DIFF_DQK, kj, one_at(DIFF_DQK))
        k2 = jnp.where(lane >= DIFF_DQK, kj, one_at(0))
        for half, kh in enumerate((k1, k2)):
            st = _dot(kh, qs_ref[:, half * TQ:(half + 1) * TQ])
            if masked:
                kpos = j * TK + lax.broadcasted_iota(jnp.int32, st.shape, 0)
                qpos = qi * TQ + lax.broadcasted_iota(jnp.int32, st.shape, 1)
                st = jnp.where(kpos <= qpos, st, -jnp.inf)
            p_ref[:, half * TQ:(half + 1) * TQ] = jnp.exp2(st).astype(BF16)

    def values(j, p_ref):
        vt = jnp.concatenate([vt_ref[0, 0, j], ones_rows], axis=0)
        acc_ref[...] += _dot(vt, p_ref[...])

    acc_ref[...] = jnp.zeros_like(acc_ref)

    def prefetch_next_step():
        load_queries(qn_ref)
        probs(0, pc_ref, False)

    @pl.when(qi == 0)
    def _():
        load_queries(q_ref)
        probs(0, pc_ref, True)
        values(0, pc_ref)
        prefetch_next_step()

    @pl.when(qi == 1)
    def _():
        probs(1, pb_ref, True)
        values(0, pc_ref)
        prefetch_next_step()
        values(1, pb_ref)

    @pl.when(qi >= 2)
    def _():
        ring = (pb_ref, pa_ref)
        probs(1, pb_ref, False)
        values(0, pc_ref)
        rest = qi - 2

        def run(b, n):
            for i in range(n):
                probs(b + i + 1, ring[(i + 1) % 2], False)
                values(b + i, ring[i % 2])

        def octet(t, carry):
            run(1 + 8 * t, 8)
            return carry

        lax.fori_loop(0, rest // 8, octet, 0)

        @pl.when(rest % 8 >= 4)
        def _():
            run(1 + 8 * (rest // 8), 4)

        b0 = 1 + 4 * (rest // 4)
        for r in range(4):
            @pl.when(rest % 4 == r)
            def _(r=r):
                run(b0, r)
                probs(b0 + r + 1, ring[(r + 1) % 2], True)
                prefetch_next_step()
                values(b0 + r, ring[r % 2])
                values(b0 + r + 1, ring[(r + 1) % 2])

    def finish(a_ref):
        lam = (jnp.exp(jnp.sum(lq1_ref[...] * lk1_ref[...], axis=-1, keepdims=True))
               - jnp.exp(jnp.sum(lq2_ref[...] * lk2_ref[...], axis=-1, keepdims=True))
               + lam_init)
        ot = a_ref[0:HEAD_W, :] / a_ref[HEAD_W:HEAD_W + 1, :]
        dt = ot[:, :TQ] - lam * ot[:, TQ:]
        ms = jnp.mean(dt * dt, axis=0, keepdims=True)
        dn = dt * lax.rsqrt(ms + HEAD_NORM_EPS)
        o_ref[0] = (dn.T * g_ref[...] * (1.0 - lam_init)).astype(BF16)

    finish(acc_ref)

    denom_min = jnp.min(acc_ref[HEAD_W:HEAD_W + 1, :])

    @pl.when(jnp.logical_not(denom_min >= DENOM_FLOOR))
    def _():
        q1, q2, _ = split_queries(q_ref)
        qs2_ref[...] = jnp.concatenate([q1, q2], axis=1)
        m2_ref[...] = jnp.full_like(m2_ref, -jnp.inf)
        acc2_ref[...] = jnp.zeros_like(acc2_ref)

        def block(j, carry):
            start = pl.multiple_of(j * TK, TK)
            s2_ref[...] = _dot(k_ref[0, pl.ds(start, TK), :], qs2_ref[...])
            st = s2_ref[...]
            kpos = j * TK + lax.broadcasted_iota(jnp.int32, st.shape, 0)
            col = lax.broadcasted_iota(jnp.int32, st.shape, 1)
            qpos = qi * TQ + jnp.where(col >= TQ, col - TQ, col)
            st = jnp.where(kpos <= qpos, st, -jnp.inf)
            m_prev = m2_ref[...]
            m_new = jnp.maximum(m_prev, jnp.max(st, axis=0, keepdims=True))
            p = jnp.exp2(st - m_new).astype(BF16)
            vt = jnp.concatenate([vt_ref[0, 0, j], ones_rows], axis=0)
            acc2_ref[...] = acc2_ref[...] * jnp.exp2(m_prev - m_new) + _dot(vt, p)
            m2_ref[...] = m_new
            return carry

        lax.fori_loop(0, qi + 1, block, 0)
        finish(acc2_ref)


def _diffattn(dq, dk, dvt, lam_vecs, g, lam_init):
    B, S, W = dq.shape
    TQ = ATT_TQ
    nq = S // TQ
    nk = dvt.shape[2]
    vec = pl.BlockSpec((1, DIFF_DQK), lambda b, h, i: (0, 0))
    return pl.pallas_call(
        functools.partial(_diffattn_kernel, lam_init=lam_init),
        grid=(B, HEADS, S // TQ),
        in_specs=[vec, vec, vec, vec,
                  pl.BlockSpec((1, TQ, HEAD_W), lambda b, h, i: (b, i, h)),
                  pl.BlockSpec((1, TQ, HEAD_W), lambda b, h, i: (b, jnp.minimum(i + 1, nq - 1), h)),
                  pl.BlockSpec((1, S, HEAD_W), lambda b, h, i: (b, 0, h)),
                  pl.BlockSpec((1, 1, nk, HEAD_W, ATT_TK), lambda b, h, i: (b, h, 0, 0, 0)),
                  pl.BlockSpec((1, HEAD_W), lambda b, h, i: (0, h))],
        out_specs=pl.BlockSpec((1, TQ, HEAD_W), lambda b, h, i: (b, i, h)),
        out_shape=jax.ShapeDtypeStruct((B, S, W), BF16),
        scratch_shapes=[pltpu.VMEM((HEAD_W, 2 * TQ), BF16),
                        pltpu.VMEM((ONES_ROWS, HEAD_W), BF16),
                        pltpu.VMEM((ATT_TK, 2 * TQ), BF16),
                        pltpu.VMEM((ATT_TK, 2 * TQ), BF16),
                        pltpu.VMEM((ATT_TK, 2 * TQ), BF16),
                        pltpu.VMEM((HEAD_W + ONES_ROWS, 2 * TQ), F32),
                        pltpu.VMEM((HEAD_W, 2 * TQ), BF16),
                        pltpu.VMEM((ATT_TK, 2 * TQ), F32),
                        pltpu.VMEM((1, 2 * TQ), F32),
                        pltpu.VMEM((HEAD_W + ONES_ROWS, 2 * TQ), F32)],
        compiler_params=pltpu.CompilerParams(
            dimension_semantics=("arbitrary", "arbitrary", "arbitrary"),
            vmem_limit_bytes=V7X_VMEM_LIMIT),
        name="diffattn",
    )(*lam_vecs, dq, dq, dk, dvt, g)


def _mixffn_kernel(x_ref, ret_ref, dif_ref, wo_ref, g1_ref, b1_ref,
                   wu_ref, cw_ref, cb_ref, wd_ref, g2_ref, b2_ref,
                   o_ref, x1_ref, carry_ref, ua0_ref, ub0_ref, ua1_ref, ub1_ref, g0_ref, g1s_ref,
                   acc_ref, *, tiles_per_seq):
    T = FFN_ROWS
    C = FFN_COLS
    u_refs = ((ua0_ref, ub0_ref), (ua1_ref, ub1_ref))
    g_refs = (g0_ref, g1s_ref)
    first_of_seq = (pl.program_id(0) % tiles_per_seq) == 0

    @pl.when(first_of_seq)
    def _():
        carry_ref[...] = jnp.zeros_like(carry_ref)

    mix = (_dot(ret_ref[...], wo_ref[0:GROUP_W, :])
           + _dot(dif_ref[...], wo_ref[GROUP_W:2 * GROUP_W, :]))
    x1 = _layer_norm(DEEPNORM_ALPHA * x_ref[...] + mix, g1_ref[...], b1_ref[...])
    x1_ref[...] = x1
    xb = x1.astype(BF16)

    starts = list(range(0, D_FF, C))
    widths = [min(C, D_FF - s0) for s0 in starts]
    n_tiles = len(starts)

    def cols(t, half):
        return slice(half * D_FF + starts[t], half * D_FF + starts[t] + widths[t])

    def up(t):
        s, w = t % 2, widths[t]
        for half in range(2):
            u = _dot(xb, wu_ref[:, cols(t, half)])
            u_refs[s][half][0:CARRY_ROWS, 0:w] = carry_ref[:, cols(t, half)]
            u_refs[s][half][CARRY_ROWS:, 0:w] = u
            carry_ref[:, cols(t, half)] = u[T - CARRY_ROWS:, :]

    def conv_half(t, half):
        ue = u_refs[t % 2][half][:, 0:widths[t]]
        cw = cw_ref[:, cols(t, half)]
        y = cw[2:3] * ue + cw[1:2] * pltpu.roll(ue, 1, 0) + cw[0:1] * pltpu.roll(ue, 2, 0)
        return y[CARRY_ROWS:, :] + cb_ref[:, cols(t, half)]

    def gate(t):
        ya = conv_half(t, 0)
        yb = conv_half(t, 1)
        g_refs[t % 2][:, 0:widths[t]] = (ya * jax.nn.sigmoid(ya) * yb).astype(BF16)

    def down(t):
        w = widths[t]
        part = _dot(g_refs[t % 2][:, 0:w], wd_ref[starts[t]:starts[t] + w, :])
        if t == 0:
            acc_ref[...] = part
        else:
            acc_ref[...] += part

    up(0)
    for t in range(n_tiles + 1):
        if t + 1 < n_tiles:
            up(t + 1)
        if t >= 1:
            down(t - 1)
        if t < n_tiles:
            gate(t)

    o_ref[...] = _layer_norm(DEEPNORM_ALPHA * x1_ref[...] + acc_ref[...],
                             g2_ref[...], b2_ref[...])


def _mixffn(x2, ret2, dif2, w_out, g1, b1, w_up, conv_w, conv_b, w_down, g2, b2, seq):
    R, D = x2.shape
    T = FFN_ROWS
    row = lambda i: (i, 0)
    resident = lambda a: pl.BlockSpec(a.shape, lambda i: (0, 0), pipeline_mode=pl.Buffered(1))
    return pl.pallas_call(
        functools.partial(_mixffn_kernel, tiles_per_seq=seq // T),
        grid=(R // T,),
        in_specs=[pl.BlockSpec((T, D), row),
                  pl.BlockSpec((T, GROUP_W), row), pl.BlockSpec((T, GROUP_W), row),
                  resident(w_out), resident(g1), resident(b1),
                  resident(w_up), resident(conv_w), resident(conv_b), resident(w_down),
                  resident(g2), resident(b2)],
        out_specs=pl.BlockSpec((T, D), row),
        out_shape=jax.ShapeDtypeStruct((R, D), F32),
        scratch_shapes=[pltpu.VMEM((T, D), F32),
                        pltpu.VMEM((CARRY_ROWS, 2 * D_FF), F32),
                        *[pltpu.VMEM((CARRY_ROWS + T, FFN_COLS), F32)] * 4,
                        *[pltpu.VMEM((T, FFN_COLS), BF16)] * 2,
                        pltpu.VMEM((T, D), F32)],
        compiler_params=pltpu.CompilerParams(
            dimension_semantics=("arbitrary",), vmem_limit_bytes=V7X_VMEM_LIMIT),
        name="mixffn",
    )(x2, ret2, dif2, w_out, g1, b1, w_up, conv_w, conv_b, w_down, g2, b2)


def _rotary_tables(seq):
    half = HEAD_W // 2
    inv_freq = ROPE_BASE ** (-jnp.arange(half, dtype=F32) / half)
    ang = jnp.arange(seq).astype(F32)[:, None] * inv_freq[None, :]
    cos, sin = jnp.cos(ang), jnp.sin(ang)
    return (jnp.concatenate([cos, cos], axis=1), jnp.concatenate([-sin, sin], axis=1),
            cos.T, sin.T)


def kernel(x, w_in, ret_gn_g, diff_norm_g, lam_q1, lam_k1, lam_q2, lam_k2, w_out,
           ln1_g, ln1_b, w_up, conv_w, conv_b, w_down, ln2_g, ln2_b):
    B, S, D = x.shape
    tabs = _rotary_tables(S)
    G = GROUP_W
    for l in range(DEPTH):
        lam_init = 0.8 - 0.6 * math.exp(-0.3 * l)
        wl = w_in[l].astype(BF16)
        rq, rkt, rv, rg, dq, dk, dvt = _proj(
            x, tabs, wl, wl[:, COL_RK * G:(COL_RK + 1) * G].T, wl[:, COL_DV * G:(COL_DV + 1) * G].T)
        ret = _retention(rq, rkt, rv, rg, ret_gn_g[l][None, :])
        lam_vecs = (lam_q1[l][None, :], lam_k1[l][None, :], lam_q2[l][None, :], lam_k2[l][None, :])
        dif = _diffattn(dq, dk, dvt, lam_vecs, diff_norm_g[l][None, :], lam_init)
        x2 = _mixffn(x.reshape(B * S, D), ret.reshape(B * S, G), dif.reshape(B * S, G),
                     w_out[l].astype(BF16), ln1_g[l][None, :], ln1_b[l][None, :],
                     w_up[l].astype(BF16), conv_w[l], conv_b[l][None, :], w_down[l].astype(BF16),
                     ln2_g[l][None, :], ln2_b[l][None, :], S)
        x = x2.reshape(B, S, D)
    return x
```

```python
import functools
import math

import jax
import jax.numpy as jnp
from jax import lax
from jax.experimental import pallas as pl
from jax.experimental.pallas import tpu as pltpu

F32 = jnp.float32
BF16 = jnp.bfloat16

D_MODEL = 1024
DEPTH = 2
HEADS = 4
HEAD_W = 128
GROUP_W = HEADS * HEAD_W
RET_CHUNK = 128
DIFF_DQK = 64
D_FF = 2816
ROPE_BASE = 10000.0
LN_EPS = 1e-5
HEAD_NORM_EPS = 1e-5
DEEPNORM_ALPHA = (2.0 * DEPTH) ** 0.25

V7X_VMEM_LIMIT = 56 * 1024 * 1024

PROJ_ROWS = 512
RET_ROWS = 2048
ATT_TK = PROJ_ROWS
ATT_TQ = ATT_TK
FFN_ROWS = 512
FFN_COLS = 1024
CARRY_ROWS = 8
ONES_ROWS = 16
DIFF_Q_SCALE = DIFF_DQK ** -0.5 * math.log2(math.e)
SHIFT_MARGIN = 1.02
DENOM_FLOOR = 2.0 ** -60

COL_RQ, COL_RK, COL_RV, COL_RG, COL_DQ, COL_DK, COL_DV = range(7)

NT_DIMS = (((1,), (1,)), ((), ()))


def _dot(a, b):
    return jnp.dot(a, b, preferred_element_type=F32)


def _dot_nt(a, b):
    return lax.dot_general(a, b, NT_DIMS, preferred_element_type=F32)


def _layer_norm(y, g, b):
    mu = jnp.mean(y, axis=-1, keepdims=True)
    yc = y - mu
    var = jnp.mean(yc * yc, axis=-1, keepdims=True)
    return yc * lax.rsqrt(var + LN_EPS) * g + b


def _proj_kernel(x_ref, cs_ref, sn_ref, cst_ref, snt_ref, w_ref, wrkt_ref, wdvt_ref,
                 rq_ref, rkt_ref, rv_ref, rg_ref, dq_ref, dk_ref, dvt_ref):
    T = PROJ_ROWS
    half = HEAD_W // 2
    xb = x_ref[0].astype(BF16)

    def group(c):
        return _dot(xb, w_ref[:, c * GROUP_W:(c + 1) * GROUP_W])

    cs = cs_ref[...]
    sn = sn_ref[...]
    t = group(COL_RQ)
    for h in range(HEADS):
        th = t[:, h * HEAD_W:(h + 1) * HEAD_W]
        rq_ref[0, :, h * HEAD_W:(h + 1) * HEAD_W] = (
            th * cs + pltpu.roll(th, half, 1) * sn).astype(BF16)
    cst = cst_ref[...]
    snt = snt_ref[...]
    tt = _dot_nt(wrkt_ref[...], xb)
    kscale = HEAD_W ** -0.5
    for h in range(HEADS):
        a = tt[h * HEAD_W:h * HEAD_W + half]
        b = tt[h * HEAD_W + half:(h + 1) * HEAD_W]
        rkt_ref[0, h * HEAD_W:h * HEAD_W + half, :] = ((a * cst - b * snt) * kscale).astype(BF16)
        rkt_ref[0, h * HEAD_W + half:(h + 1) * HEAD_W, :] = ((a * snt + b * cst) * kscale).astype(BF16)
    rv_ref[0] = group(COL_RV).astype(BF16)
    rg_ref[0] = group(COL_RG)
    dq_ref[0] = (group(COL_DQ) * DIFF_Q_SCALE).astype(BF16)
    dk_ref[0] = group(COL_DK).astype(BF16)
    dvt = _dot_nt(wdvt_ref[...], xb).astype(BF16)
    dvt_ref[0, :, 0] = dvt.reshape(HEADS, HEAD_W, T)


def _proj(x, tabs, w, wrkt, wdvt):
    B, S, D = x.shape
    T = PROJ_ROWS
    n = S // T
    row = lambda b, i: (b, i, 0)
    const2 = lambda b, i: (0, 0)
    wspec = lambda a: pl.BlockSpec(a.shape, const2)
    out_shapes = (
        jax.ShapeDtypeStruct((B, S, GROUP_W), BF16),
        jax.ShapeDtypeStruct((B, GROUP_W, S), BF16),
        jax.ShapeDtypeStruct((B, S, GROUP_W), BF16),
        jax.ShapeDtypeStruct((B, S, GROUP_W), F32),
        jax.ShapeDtypeStruct((B, S, GROUP_W), BF16),
        jax.ShapeDtypeStruct((B, S, GROUP_W), BF16),
        jax.ShapeDtypeStruct((B, HEADS, n, HEAD_W, T), BF16),
    )
    rows = pl.BlockSpec((1, T, GROUP_W), row)
    return pl.pallas_call(
        _proj_kernel,
        grid=(B, n),
        in_specs=[pl.BlockSpec((1, T, D), row),
                  pl.BlockSpec((T, HEAD_W), lambda b, i: (i, 0)),
                  pl.BlockSpec((T, HEAD_W), lambda b, i: (i, 0)),
                  pl.BlockSpec((HEAD_W // 2, T), lambda b, i: (0, i)),
                  pl.BlockSpec((HEAD_W // 2, T), lambda b, i: (0, i)),
                  wspec(w), wspec(wrkt), wspec(wdvt)],
        out_specs=(
            rows,
            pl.BlockSpec((1, GROUP_W, T), lambda b, i: (b, 0, i)),
            rows, rows, rows, rows,
            pl.BlockSpec((1, HEADS, 1, HEAD_W, T), lambda b, i: (b, 0, i, 0, 0)),
        ),
        out_shape=out_shapes,
        compiler_params=pltpu.CompilerParams(
            dimension_semantics=("arbitrary", "arbitrary"),
            vmem_limit_bytes=V7X_VMEM_LIMIT),
        name="proj",
    )(x, *tabs, w, wrkt, wdvt)


def _retention_kernel(q_ref, kt_ref, v_ref, g_ref, gn_ref, o_ref, state_ref):
    C = RET_CHUNK

    @pl.when(pl.program_id(1) == 0)
    def _():
        state_ref[...] = jnp.zeros_like(state_ref)

    ri = lax.broadcasted_iota(jnp.int32, (C, C), 0)
    ci = lax.broadcasted_iota(jnp.int32, (C, C), 1)
    rel = (ri - ci).astype(F32)
    gn = gn_ref[...]
    for h in range(HEADS):
        log_g = math.log(1.0 - 2.0 ** (-5.0 - h))
        d_intra = jnp.where(ri >= ci, jnp.exp(log_g * jnp.maximum(rel, 0.0)), 0.0)
        q_dec = jnp.exp(log_g * (ri.astype(F32) + 1.0))
        k_dec = jnp.exp(log_g * (C - 1.0 - ci.astype(F32)))
        c_dec = math.exp(log_g * C)
        cols = slice(h * HEAD_W, (h + 1) * HEAD_W)
        state = state_ref[h]
        for c in range(RET_ROWS // C):
            rws = slice(c * C, (c + 1) * C)
            qc = q_ref[0, rws, cols]
            ktc = kt_ref[0, cols, rws]
            vc = v_ref[0, rws, cols]
            s = _dot(qc, ktc) * d_intra
            inner = _dot(s.astype(BF16), vc)
            cross = _dot((qc.astype(F32) * q_dec).astype(BF16), state.astype(BF16))
            state = state * c_dec + _dot((ktc.astype(F32) * k_dec).astype(BF16), vc)
            y = inner + cross
            mu = jnp.mean(y, axis=-1, keepdims=True)
            yc = y - mu
            var = jnp.mean(yc * yc, axis=-1, keepdims=True)
            yn = yc * lax.rsqrt(var + HEAD_NORM_EPS) * gn[:, cols]
            gate = g_ref[0, rws, cols]
            o_ref[0, rws, cols] = (yn * (gate * jax.nn.sigmoid(gate))).astype(BF16)
        state_ref[h] = state


def _retention(rq, rkt, rv, rg, gn):
    B, S, W = rq.shape
    T = RET_ROWS
    row = lambda b, i: (b, i, 0)
    rows = pl.BlockSpec((1, T, W), row)
    return pl.pallas_call(
        _retention_kernel,
        grid=(B, S // T),
        in_specs=[rows, pl.BlockSpec((1, W, T), lambda b, i: (b, 0, i)), rows, rows,
                  pl.BlockSpec((1, W), lambda b, i: (0, 0))],
        out_specs=rows,
        out_shape=jax.ShapeDtypeStruct((B, S, W), BF16),
        scratch_shapes=[pltpu.VMEM((HEADS, HEAD_W, HEAD_W), F32)],
        compiler_params=pltpu.CompilerParams(
            dimension_semantics=("arbitrary", "arbitrary"),
            vmem_limit_bytes=V7X_VMEM_LIMIT),
        name="retention",
    )(rq, rkt, rv, rg, gn)


def _diffattn_kernel(lq1_ref, lk1_ref, lq2_ref, lk2_ref, q_ref, qn_ref, k_ref, vt_ref, g_ref,
                     o_ref, qs_ref, kabs_ref, pa_ref, pb_ref, pc_ref, acc_ref,
                     qs2_ref, s2_ref, m2_ref, acc2_ref, *, lam_init):
    TQ, TK = ATT_TQ, ATT_TK
    qi = pl.program_id(2)
    lane = lax.broadcasted_iota(jnp.int32, (TK, HEAD_W), 1)
    one_at = lambda c: jnp.where(lane == c, 1.0, 0.0).astype(BF16)
    ones_rows = jnp.where(lax.broadcasted_iota(jnp.int32, (ONES_ROWS, TK), 0) == 0,
                          1.0, 0.0).astype(BF16)

    def split_queries(src_ref):
        qt = src_ref[0].astype(F32).T
        chan = lax.broadcasted_iota(jnp.int32, qt.shape, 0)
        return (jnp.where(chan < DIFF_DQK, qt, 0.0).astype(BF16),
                jnp.where(chan >= DIFF_DQK, qt, 0.0).astype(BF16), chan)

    @pl.when(qi == 0)
    def _():
        kabs = jnp.max(jnp.abs(k_ref[0].astype(F32)), axis=0, keepdims=True)
        kabs_ref[...] = jnp.broadcast_to(kabs, kabs_ref.shape).astype(BF16)

    def load_queries(src_ref):
        q1, q2, chan = split_queries(src_ref)
        for half, (qh, shift_row) in enumerate(((q1, DIFF_DQK), (q2, 0))):
            bound = _dot(kabs_ref[...], jnp.abs(qh))[0:1]
            shift = (bound * SHIFT_MARGIN).astype(BF16).astype(F32)
            w = jnp.where(chan == shift_row, -shift, qh.astype(F32))
            qs_ref[:, half * TQ:(half + 1) * TQ] = w.astype(BF16)

    def probs(j, p_ref, masked):
        start = pl.multiple_of(j * TK, TK)
        kj = k_ref[0, pl.ds(start, TK), :]
        k1 = jnp.where(lane < DIFF_DQK, kj, one_at(DIFF_DQK))
        k2 = jnp.where(lane >= DIFF_DQK, kj, one_at(0))
        for half, kh in enumerate((k1, k2)):
            st = _dot(kh, qs_ref[:, half * TQ:(half + 1) * TQ])
            if masked:
                kpos = j * TK + lax.broadcasted_iota(jnp.int32, st.shape, 0)
                qpos = qi * TQ + lax.broadcasted_iota(jnp.int32, st.shape, 1)
                st = jnp.where(kpos <= qpos, st, -jnp.inf)
            p_ref[:, half * TQ:(half + 1) * TQ] = jnp.exp2(st).astype(BF16)

    def values(j, p_ref):
        vt = jnp.concatenate([vt_ref[0, 0, j], ones_rows], axis=0)
        acc_ref[...] += _dot(vt, p_ref[...])

    acc_ref[...] = jnp.zeros_like(acc_ref)

    def prefetch_next_step():
        load_queries(qn_ref)
        probs(0, pc_ref, False)

    @pl.when(qi == 0)
    def _():
        load_queries(q_ref)
        probs(0, pc_ref, True)
        values(0, pc_ref)
        prefetch_next_step()

    @pl.when(qi == 1)
    def _():
        probs(1, pb_ref, True)
        values(0, pc_ref)
        prefetch_next_step()
        values(1, pb_ref)

    @pl.when(qi >= 2)
    def _():
        ring = (pb_ref, pa_ref)
        probs(1, pb_ref, False)
        values(0, pc_ref)
        rest = qi - 2

        def run(b, n):
            for i in range(n):
                probs(b + i + 1, ring[(i + 1) % 2], False)
                values(b + i, ring[i % 2])

        def octet(t, carry):
            run(1 + 8 * t, 8)
            return carry

        lax.fori_loop(0, rest // 8, octet, 0)

        @pl.when(rest % 8 >= 4)
        def _():
            run(1 + 8 * (rest // 8), 4)

        b0 = 1 + 4 * (rest // 4)
        for r in range(4):
            @pl.when(rest % 4 == r)
            def _(r=r):
                run(b0, r)
                probs(b0 + r + 1, ring[(r + 1) % 2], True)
                prefetch_next_step()
                values(b0 + r, ring[r % 2])
                values(b0 + r + 1, ring[(r + 1) % 2])

    def finish(a_ref):
        lam = (jnp.exp(jnp.sum(lq1_ref[...] * lk1_ref[...], axis=-1, keepdims=True))
               - jnp.exp(jnp.sum(lq2_ref[...] * lk2_ref[...], axis=-1, keepdims=True))
               + lam_init)
        ot = a_ref[0:HEAD_W, :] / a_ref[HEAD_W:HEAD_W + 1, :]
        dt = ot[:, :TQ] - lam * ot[:, TQ:]
        ms = jnp.mean(dt * dt, axis=0, keepdims=True)
        dn = dt * lax.rsqrt(ms + HEAD_NORM_EPS)
        o_ref[0] = (dn.T * g_ref[...] * (1.0 - lam_init)).astype(BF16)

    finish(acc_ref)

    denom_min = jnp.min(acc_ref[HEAD_W:HEAD_W + 1, :])

    @pl.when(jnp.logical_not(denom_min >= DENOM_FLOOR))
    def _():
        q1, q2, _ = split_queries(q_ref)
        qs2_ref[...] = jnp.concatenate([q1, q2], axis=1)
        m2_ref[...] = jnp.full_like(m2_ref, -jnp.inf)
        acc2_ref[...] = jnp.zeros_like(acc2_ref)

        def block(j, carry):
            start = pl.multiple_of(j * TK, TK)
            s2_ref[...] = _dot(k_ref[0, pl.ds(start, TK), :], qs2_ref[...])
            st = s2_ref[...]
            kpos = j * TK + lax.broadcasted_iota(jnp.int32, st.shape, 0)
            col = lax.broadcasted_iota(jnp.int32, st.shape, 1)
            qpos = qi * TQ + jnp.where(col >= TQ, col - TQ, col)
            st = jnp.where(kpos <= qpos, st, -jnp.inf)
            m_prev = m2_ref[...]
            m_new = jnp.maximum(m_prev, jnp.max(st, axis=0, keepdims=True))
            p = jnp.exp2(st - m_new).astype(BF16)
            vt = jnp.concatenate([vt_ref[0, 0, j], ones_rows], axis=0)
            acc2_ref[...] = acc2_ref[...] * jnp.exp2(m_prev - m_new) + _dot(vt, p)
            m2_ref[...] = m_new
            return carry

        lax.fori_loop(0, qi + 1, block, 0)
        finish(acc2_ref)


def _diffattn(dq, dk, dvt, lam_vecs, g, lam_init):
    B, S, W = dq.shape
    TQ = ATT_TQ
    nq = S // TQ
    nk = dvt.shape[2]
    vec = pl.BlockSpec((1, DIFF_DQK), lambda b, h, i: (0, 0))
    return pl.pallas_call(
        functools.partial(_diffattn_kernel, lam_init=lam_init),
        grid=(B, HEADS, S // TQ),
        in_specs=[vec, vec, vec, vec,
                  pl.BlockSpec((1, TQ, HEAD_W), lambda b, h, i: (b, i, h)),
                  pl.BlockSpec((1, TQ, HEAD_W), lambda b, h, i: (b, jnp.minimum(i + 1, nq - 1), h)),
                  pl.BlockSpec((1, S, HEAD_W), lambda b, h, i: (b, 0, h)),
                  pl.BlockSpec((1, 1, nk, HEAD_W, ATT_TK), lambda b, h, i: (b, h, 0, 0, 0)),
                  pl.BlockSpec((1, HEAD_W), lambda b, h, i: (0, h))],
        out_specs=pl.BlockSpec((1, TQ, HEAD_W), lambda b, h, i: (b, i, h)),
        out_shape=jax.ShapeDtypeStruct((B, S, W), BF16),
        scratch_shapes=[pltpu.VMEM((HEAD_W, 2 * TQ), BF16),
                        pltpu.VMEM((ONES_ROWS, HEAD_W), BF16),
                        pltpu.VMEM((ATT_TK, 2 * TQ), BF16),
                        pltpu.VMEM((ATT_TK, 2 * TQ), BF16),
                        pltpu.VMEM((ATT_TK, 2 * TQ), BF16),
                        pltpu.VMEM((HEAD_W + ONES_ROWS, 2 * TQ), F32),
                        pltpu.VMEM((HEAD_W, 2 * TQ), BF16),
                        pltpu.VMEM((ATT_TK, 2 * TQ), F32),
                        pltpu.VMEM((1, 2 * TQ), F32),
                        pltpu.VMEM((HEAD_W + ONES_ROWS, 2 * TQ), F32)],
        compiler_params=pltpu.CompilerParams(
            dimension_semantics=("arbitrary", "arbitrary", "arbitrary"),
            vmem_limit_bytes=V7X_VMEM_LIMIT),
        name="diffattn",
    )(*lam_vecs, dq, dq, dk, dvt, g)


def _mixffn_kernel(x_ref, ret_ref, dif_ref, wo_ref, g1_ref, b1_ref,
                   wu_ref, cw_ref, cb_ref, wd_ref, g2_ref, b2_ref,
                   o_ref, x1_ref, carry_ref, ua0_ref, ub0_ref, ua1_ref, ub1_ref, g0_ref, g1s_ref,
                   acc_ref, *, tiles_per_seq):
    T = FFN_ROWS
    C = FFN_COLS
    u_refs = ((ua0_ref, ub0_ref), (ua1_ref, ub1_ref))
    g_refs = (g0_ref, g1s_ref)
    first_of_seq = (pl.program_id(0) % tiles_per_seq) == 0

    @pl.when(first_of_seq)
    def _():
        carry_ref[...] = jnp.zeros_like(carry_ref)

    mix = (_dot(ret_ref[...], wo_ref[0:GROUP_W, :])
           + _dot(dif_ref[...], wo_ref[GROUP_W:2 * GROUP_W, :]))
    x1 = _layer_norm(DEEPNORM_ALPHA * x_ref[...] + mix, g1_ref[...], b1_ref[...])
    x1_ref[...] = x1
    xb = x1.astype(BF16)

    starts = list(range(0, D_FF, C))
    widths = [min(C, D_FF - s0) for s0 in starts]
    n_tiles = len(starts)

    def cols(t, half):
        return slice(half * D_FF + starts[t], half * D_FF + starts[t] + widths[t])

    def up(t):
        s, w = t % 2, widths[t]
        for half in range(2):
            u = _dot(xb, wu_ref[:, cols(t, half)])
            u_refs[s][half][0:CARRY_ROWS, 0:w] = carry_ref[:, cols(t, half)]
            u_refs[s][half][CARRY_ROWS:, 0:w] = u
            carry_ref[:, cols(t, half)] = u[T - CARRY_ROWS:, :]

    def conv_half(t, half):
        ue = u_refs[t % 2][half][:, 0:widths[t]]
        cw = cw_ref[:, cols(t, half)]
        y = cw[2:3] * ue + cw[1:2] * pltpu.roll(ue, 1, 0) + cw[0:1] * pltpu.roll(ue, 2, 0)
        return y[CARRY_ROWS:, :] + cb_ref[:, cols(t, half)]

    def gate(t):
        ya = conv_half(t, 0)
        yb = conv_half(t, 1)
        g_refs[t % 2][:, 0:widths[t]] = (ya * jax.nn.sigmoid(ya) * yb).astype(BF16)

    def down(t):
        w = widths[t]
        part = _dot(g_refs[t % 2][:, 0:w], wd_ref[starts[t]:starts[t] + w, :])
        if t == 0:
            acc_ref[...] = part
        else:
            acc_ref[...] += part

    up(0)
    for t in range(n_tiles + 1):
        if t + 1 < n_tiles:
            up(t + 1)
        if t >= 1:
            down(t - 1)
        if t < n_tiles:
            gate(t)

    o_ref[...] = _layer_norm(DEEPNORM_ALPHA * x1_ref[...] + acc_ref[...],
                             g2_ref[...], b2_ref[...])


def _mixffn(x2, ret2, dif2, w_out, g1, b1, w_up, conv_w, conv_b, w_down, g2, b2, seq):
    R, D = x2.shape
    T = FFN_ROWS
    row = lambda i: (i, 0)
    resident = lambda a: pl.BlockSpec(a.shape, lambda i: (0, 0), pipeline_mode=pl.Buffered(1))
    return pl.pallas_call(
        functools.partial(_mixffn_kernel, tiles_per_seq=seq // T),
        grid=(R // T,),
        in_specs=[pl.BlockSpec((T, D), row),
                  pl.BlockSpec((T, GROUP_W), row), pl.BlockSpec((T, GROUP_W), row),
                  resident(w_out), resident(g1), resident(b1),
                  resident(w_up), resident(conv_w), resident(conv_b), resident(w_down),
                  resident(g2), resident(b2)],
        out_specs=pl.BlockSpec((T, D), row),
        out_shape=jax.ShapeDtypeStruct((R, D), F32),
        scratch_shapes=[pltpu.VMEM((T, D), F32),
                        pltpu.VMEM((CARRY_ROWS, 2 * D_FF), F32),
                        *[pltpu.VMEM((CARRY_ROWS + T, FFN_COLS), F32)] * 4,
                        *[pltpu.VMEM((T, FFN_COLS), BF16)] * 2,
                        pltpu.VMEM((T, D), F32)],
        compiler_params=pltpu.CompilerParams(
            dimension_semantics=("arbitrary",), vmem_limit_bytes=V7X_VMEM_LIMIT),
        name="mixffn",
    )(x2, ret2, dif2, w_out, g1, b1, w_up, conv_w, conv_b, w_down, g2, b2)


def _rotary_tables(seq):
    half = HEAD_W // 2
    inv_freq = ROPE_BASE ** (-jnp.arange(half, dtype=F32) / half)
    ang = jnp.arange(seq).astype(F32)[:, None] * inv_freq[None, :]
    cos, sin = jnp.cos(ang), jnp.sin(ang)
    return (jnp.concatenate([cos, cos], axis=1), jnp.concatenate([-sin, sin], axis=1),
            cos.T, sin.T)


def kernel(x, w_in, ret_gn_g, diff_norm_g, lam_q1, lam_k1, lam_q2, lam_k2, w_out,
           ln1_g, ln1_b, w_up, conv_w, conv_b, w_down, ln2_g, ln2_b):
    B, S, D = x.shape
    tabs = _rotary_tables(S)
    G = GROUP_W
    for l in range(DEPTH):
        lam_init = 0.8 - 0.6 * math.exp(-0.3 * l)
        wl = w_in[l].astype(BF16)
        rq, rkt, rv, rg, dq, dk, dvt = _proj(
            x, tabs, wl, wl[:, COL_RK * G:(COL_RK + 1) * G].T, wl[:, COL_DV * G:(COL_DV + 1) * G].T)
        ret = _retention(rq, rkt, rv, rg, ret_gn_g[l][None, :])
        lam_vecs = (lam_q1[l][None, :], lam_k1[l][None, :], lam_q2[l][None, :], lam_k2[l][None, :])
        dif = _diffattn(dq, dk, dvt, lam_vecs, diff_norm_g[l][None, :], lam_init)
        x2 = _mixffn(x.reshape(B * S, D), ret.reshape(B * S, G), dif.reshape(B * S, G),
                     w_out[l].astype(BF16), ln1_g[l][None, :], ln1_b[l][None, :],
                     w_up[l].astype(BF16), conv_w[l], conv_b[l][None, :], w_down[l].astype(BF16),
                     ln2_g[l][None, :], ln2_b[l][None, :], S)
        x = x2.reshape(B, S, D)
    return x
```
